```python
import math
import jax, jax.numpy as jnp
from jax import lax
import numpy as np

D_MODEL = 1024
BATCH = 2
SEQ = 8192
DEPTH = 1

CHUNK = 64
Q_BLOCK = 128
DA_HEADS = 8
DA_DH = 64
DA_DV = 2 * DA_DH
DA_QK = DA_HEADS * 2 * DA_DH
DA_V = DA_HEADS * DA_DV
CB_HEADS = 8
CB_DH = 64
CB_W = CB_HEADS * CB_DH
CB_LEFT = 8
CB_BAND = (CB_LEFT + 1) * CHUNK
REL_CLIP = 256
REL_SIZE = REL_CLIP + CHUNK
N_BRANCH = 2
D_FF = 2816
CONV_W = 3
ROPE_THETA = 10000.0
LN_EPS = 1e-5
ALPHA = (2 * DEPTH) ** 0.25
BETA = (8 * DEPTH) ** -0.25
IN_SPLITS = [DA_QK, DA_QK, DA_V, CB_W, CB_W, CB_W, N_BRANCH * D_MODEL]

kernel_name = "hybrid_diffattn_chunkband_convffn_deepnorm"


def layer_norm(x, g, b):
    xf = x.astype(jnp.float32)
    mu = jnp.mean(xf, axis=-1, keepdims=True)
    var = jnp.mean(jnp.square(xf - mu), axis=-1, keepdims=True)
    y = (xf - mu) * lax.rsqrt(var + LN_EPS)
    return (y * g.astype(jnp.float32) + b.astype(jnp.float32)).astype(x.dtype)


def rope(t, positions):
    dh = t.shape[-1]
    inv = 1.0 / (ROPE_THETA ** (jnp.arange(0, dh, 2, dtype=jnp.float32) / dh))
    ang = positions.astype(jnp.float32)[..., None] * inv
    cos = jnp.cos(ang)[:, :, None, None, :]
    sin = jnp.sin(ang)[:, :, None, None, :]
    tf = t.astype(jnp.float32)
    t1, t2 = tf[..., : dh // 2], tf[..., dh // 2:]
    out = jnp.concatenate([t1 * cos - t2 * sin, t2 * cos + t1 * sin], axis=-1)
    return out.astype(t.dtype)


def diff_attention(q, k, v, lam, lam_init, subln_g):
    B, S, H, _, dh = q.shape
    nqb = S // Q_BLOCK
    qb = q.reshape(B, nqb, Q_BLOCK, H, 2, dh).transpose(1, 0, 2, 3, 4, 5)
    key_chunk = jnp.arange(S) // CHUNK
    scale = dh ** -0.5

    def block(args):
        qi, i = args
        s = jnp.einsum('bqhtd,bkhtd->bhtqk', qi, k).astype(jnp.float32) * scale
        q_chunk = (i * Q_BLOCK + jnp.arange(Q_BLOCK)) // CHUNK
        mask = key_chunk[None, :] <= q_chunk[:, None]
        s = jnp.where(mask, s, -1e30)
        p = jax.nn.softmax(s, axis=-1)
        w = p[:, :, 0] - lam * p[:, :, 1]
        return jnp.einsum('bhqk,bkhe->bqhe', w.astype(v.dtype), v)

    o = lax.map(block, (qb, jnp.arange(nqb)))
    o = o.transpose(1, 0, 2, 3, 4).reshape(B, S, H, -1)
    of = o.astype(jnp.float32)
    of = of * lax.rsqrt(jnp.mean(jnp.square(of), axis=-1, keepdims=True) + LN_EPS)
    of = of * subln_g.astype(jnp.float32) * (1.0 - lam_init)
    return of.reshape(B, S, -1).astype(v.dtype)


def chunk_band_attention(q, k, v, rel_bias):
    B, S, H, dh = q.shape
    NC = S // CHUNK
    qc = q.reshape(B, NC, CHUNK, H, dh)

    def band(t):
        tc = t.reshape(B, NC, CHUNK, H, dh)
        tp = jnp.pad(tc, ((0, 0), (CB_LEFT, 0), (0, 0), (0, 0), (0, 0)))
        return jnp.concatenate([tp[:, j:j + NC] for j in range(CB_LEFT + 1)], axis=2)

    kb, vb = band(k), band(v)
    slot = jnp.arange(CB_BAND)
    src_chunk = jnp.arange(NC)[:, None] - CB_LEFT + slot[None, :] // CHUNK
    valid = src_chunk >= 0
    rel = slot[None, :] - CB_LEFT * CHUNK - jnp.arange(CHUNK)[:, None]
    idx = jnp.clip(rel, -REL_CLIP, CHUNK - 1) + REL_CLIP
    bias = rel_bias.astype(jnp.float32)[:, idx]
    s = jnp.einsum('bnqhd,bnkhd->bhnqk', qc, kb).astype(jnp.float32) * (dh ** -0.5)
    s = s + bias[None, :, None]
    s = jnp.where(valid[None, None, :, None, :], s, -1e30)
    p = jax.nn.softmax(s, axis=-1)
    o = jnp.einsum('bhnqk,bnkhd->bnqhd', p.astype(v.dtype), vb)
    return o.reshape(B, S, H * dh)


def token_mixer(x, positions, w_in, b_gate, lam, lam_init, subln_g, rel_bias,
                w_proj_a, w_proj_b, w_out):
    B, S, D = x.shape
    proj = x @ w_in
    offs = np.cumsum(IN_SPLITS)[:-1].tolist()
    q_a, k_a, v_a, q_b, k_b, v_b, gates = jnp.split(proj, offs, axis=-1)
    q_a = rope(q_a.reshape(B, S, DA_HEADS, 2, DA_DH), positions)
    k_a = rope(k_a.reshape(B, S, DA_HEADS, 2, DA_DH), positions)
    v_a = v_a.reshape(B, S, DA_HEADS, DA_DV)
    y_a = diff_attention(q_a, k_a, v_a, lam, lam_init, subln_g) @ w_proj_a
    y_b = chunk_band_attention(q_b.reshape(B, S, CB_HEADS, CB_DH),
                               k_b.reshape(B, S, CB_HEADS, CB_DH),
                               v_b.reshape(B, S, CB_HEADS, CB_DH), rel_bias) @ w_proj_b
    g = jax.nn.sigmoid((gates + b_gate).astype(jnp.float32)).astype(x.dtype)
    g = g.reshape(B, S, N_BRANCH, D)
    merged = g[:, :, 0] * y_a + g[:, :, 1] * y_b
    return merged @ w_out


def conv_ffn(x, w_up, conv_w, conv_b, w_down):
    u = x @ w_up
    C = u.shape[-1]
    c = lax.conv_general_dilated(u, conv_w[:, None, :].astype(u.dtype), window_strides=(1,),
                                 padding=[(CONV_W - 1, 0)],
                                 dimension_numbers=('NWC', 'WIO', 'NWC'),
                                 feature_group_count=C) + conv_b
    gate, val = jnp.split(c, 2, axis=-1)
    return (jax.nn.silu(gate) * val) @ w_down


def setup_inputs(seed: int = 0) -> dict:
    key = jax.random.key(seed)
    ks = jax.random.split(key, 32)
    nrm = lambda k, shape, s: jax.random.normal(k, shape, jnp.float32) * s
    D = D_MODEL
    x = jax.random.normal(ks[0], (BATCH, SEQ, D), jnp.float32)
    positions = jnp.broadcast_to(jnp.arange(SEQ, dtype=jnp.int32)[None, :], (BATCH, SEQ))
    sd = D ** -0.5
    w_in = jnp.concatenate([
        nrm(ks[1], (DEPTH, D, DA_QK), sd),
        nrm(ks[2], (DEPTH, D, DA_QK), sd),
        nrm(ks[3], (DEPTH, D, DA_V), BETA * sd),
        nrm(ks[4], (DEPTH, D, CB_W), sd),
        nrm(ks[5], (DEPTH, D, CB_W), sd),
        nrm(ks[6], (DEPTH, D, CB_W), BETA * sd),
        nrm(ks[7], (DEPTH, D, N_BRANCH * D), sd),
    ], axis=-1)
    return {
        "x": x,
        "positions": positions,
        "w_in": w_in,
        "b_gate": nrm(ks[8], (DEPTH, N_BRANCH * D), 0.01),
        "lambda_q1": nrm(ks[9], (DEPTH, DA_DH), 0.1),
        "lambda_k1": nrm(ks[10], (DEPTH, DA_DH), 0.1),
        "lambda_q2": nrm(ks[11], (DEPTH, DA_DH), 0.1),
        "lambda_k2": nrm(ks[12], (DEPTH, DA_DH), 0.1),
        "subln_g": 1.0 + nrm(ks[13], (DEPTH, DA_DV), 0.01),
        "rel_bias": nrm(ks[14], (DEPTH, CB_HEADS, REL_SIZE), 0.1),
        "w_proj_a": nrm(ks[15], (DEPTH, DA_V, D), BETA * DA_V ** -0.5),
        "w_proj_b": nrm(ks[16], (DEPTH, CB_W, D), BETA * CB_W ** -0.5),
        "w_out": nrm(ks[17], (DEPTH, D, D), BETA * sd),
        "ln1_g": 1.0 + nrm(ks[18], (DEPTH, D), 0.01),
        "ln1_b": nrm(ks[19], (DEPTH, D), 0.01),
        "w_up": nrm(ks[20], (DEPTH, D, 2 * D_FF), sd),
        "conv_w": nrm(ks[21], (DEPTH, CONV_W, 2 * D_FF), CONV_W ** -0.5),
        "conv_b": nrm(ks[22], (DEPTH, 2 * D_FF), 0.01),
        "w_down": nrm(ks[23], (DEPTH, D_FF, D), BETA * D_FF ** -0.5),
        "ln2_g": 1.0 + nrm(ks[24], (DEPTH, D), 0.01),
        "ln2_b": nrm(ks[25], (DEPTH, D), 0.01),
    }


def reference(x, positions, w_in, b_gate, lambda_q1, lambda_k1, lambda_q2, lambda_k2,
              subln_g, rel_bias, w_proj_a, w_proj_b, w_out, ln1_g, ln1_b,
              w_up, conv_w, conv_b, w_down, ln2_g, ln2_b):
    h = x
    for l in range(DEPTH):
        lam_init = 0.8 - 0.6 * math.exp(-0.3 * l)
        lam = (jnp.exp(jnp.sum(lambda_q1[l].astype(jnp.float32) * lambda_k1[l].astype(jnp.float32)))
               - jnp.exp(jnp.sum(lambda_q2[l].astype(jnp.float32) * lambda_k2[l].astype(jnp.float32)))
               + lam_init)
        mix = token_mixer(h, positions, w_in[l], b_gate[l], lam, lam_init, subln_g[l], rel_bias[l],
                          w_proj_a[l], w_proj_b[l], w_out[l])
        h = layer_norm(ALPHA * h + mix, ln1_g[l], ln1_b[l])
        h = layer_norm(ALPHA * h + conv_ffn(h, w_up[l], conv_w[l], conv_b[l], w_down[l]), ln2_g[l], ln2_b[l])
    return h
```

```python
import functools
import math

import jax
import jax.numpy as jnp
from jax import lax
from jax.experimental import pallas as pl
from jax.experimental.pallas import tpu as pltpu

F32 = jnp.float32
BF16 = jnp.bfloat16

LANES = 128
CHUNK = 64
DA_HEADS = 8
DA_DH = 64
DA_HW = 2 * DA_DH
CB_HEADS = 8
CB_DH = 64
CB_LEFT = 8
REL_CLIP = 256
ROPE_THETA = 10000.0
LN_EPS = 1e-5
CONV_W = 3
NEG = -1e30

VMEM_LIMIT = 56 * 1024 * 1024

ROW_TILE = 512
ATT_TILE = 512
BAND_TQ = 256
BAND_WIN = BAND_TQ + CB_LEFT * CHUNK
BAND_EXT = BAND_WIN + CB_LEFT * CHUNK
FF_CHUNK = 256
HALO = 8


def _dot(a, b):
    return jnp.dot(a, b, preferred_element_type=F32)


def _const_spec(shape):
    nd = len(shape)
    return pl.BlockSpec(shape, lambda *_: (0,) * nd, pipeline_mode=pl.Buffered(1))


def _params(n_axes):
    return pltpu.CompilerParams(dimension_semantics=("arbitrary",) * n_axes,
                                vmem_limit_bytes=VMEM_LIMIT)


def _inproj_kernel(x_ref, pos_ref, inv_ref, w_ref,
                   qt_ref, k_ref, vt_ref, qb_ref, kb_ref, vb_ref, *, scale_a, scale_b):
    tm = x_ref.shape[0]
    xb = x_ref[...].astype(BF16)
    ang = pos_ref[...].astype(F32) * inv_ref[...]
    cos = jnp.cos(ang)
    sin = jnp.sin(ang)
    lane = lax.broadcasted_iota(jnp.int32, (tm, LANES), 1)
    first_half = (lane % DA_DH) < (DA_DH // 2)
    sin = jnp.where(first_half, -sin, sin)

    def rope(t):
        partner = jnp.where(first_half,
                            pltpu.roll(t, LANES - DA_DH // 2, axis=1),
                            pltpu.roll(t, DA_DH // 2, axis=1))
        return t * cos + partner * sin

    grp = 4 * DA_HW
    qk_w = DA_HEADS * DA_HW
    for g in range(qk_w // grp):
        t = _dot(xb, w_ref[:, g * grp:(g + 1) * grp])
        for j in range(4):
            r = rope(t[:, j * DA_HW:(j + 1) * DA_HW]) * scale_a
            qt_ref[0, g * 4 + j] = r.T.astype(BF16)
    for g in range(qk_w // grp):
        t = _dot(xb, w_ref[:, qk_w + g * grp: qk_w + (g + 1) * grp])
        for j in range(4):
            c0 = g * grp + j * DA_HW
            k_ref[:, c0:c0 + DA_HW] = rope(t[:, j * DA_HW:(j + 1) * DA_HW]).astype(BF16)
    for g in range(qk_w // grp):
        t = _dot(xb, w_ref[:, 2 * qk_w + g * grp: 2 * qk_w + (g + 1) * grp])
        for j in range(4):
            vt_ref[0, g * 4 + j] = t[:, j * DA_HW:(j + 1) * DA_HW].T.astype(BF16)
    off = 3 * qk_w
    cbw = CB_HEADS * CB_DH
    qb_ref[...] = (_dot(xb, w_ref[:, off:off + cbw]) * scale_b).astype(BF16)
    kb_ref[...] = _dot(xb, w_ref[:, off + cbw:off + 2 * cbw]).astype(BF16)
    vb_ref[...] = _dot(xb, w_ref[:, off + 2 * cbw:off + 3 * cbw]).astype(BF16)


def _inproj(x2, pos2, inv, w_attn, B, S):
    T, D = x2.shape
    tm = ROW_TILE
    spb = S // tm
    qk_w = DA_HEADS * DA_HW
    cbw = CB_HEADS * CB_DH
    t_spec = pl.BlockSpec((1, DA_HEADS, DA_HW, tm), lambda i: (i // spb, 0, 0, i % spb))
    row = lambda w: pl.BlockSpec((tm, w), lambda i: (i, 0))
    t_shape = jax.ShapeDtypeStruct((B, DA_HEADS, DA_HW, S), BF16)
    return pl.pallas_call(
        functools.partial(_inproj_kernel, scale_a=DA_DH ** -0.5, scale_b=CB_DH ** -0.5),
        grid=(T // tm,),
        in_specs=[row(D), row(1), _const_spec((1, LANES)), _const_spec(w_attn.shape)],
        out_specs=[t_spec, row(qk_w), t_spec, row(cbw), row(cbw), row(cbw)],
        out_shape=[t_shape, jax.ShapeDtypeStruct((T, qk_w), BF16), t_shape,
                   jax.ShapeDtypeStruct((T, cbw), BF16), jax.ShapeDtypeStruct((T, cbw), BF16),
                   jax.ShapeDtypeStruct((T, cbw), BF16)],
        compiler_params=_params(1),
        name="inproj",
    )(x2, pos2, inv, w_attn)


def _diffattn_kernel(qt_ref, k_ref, vt_ref, lq1_ref, lk1_ref, lq2_ref, lk2_ref, g_ref,
                     o_ref, m_ref, l_ref, acc_ref, *, lam_init):
    qi = pl.program_id(2)
    tq = qt_ref.shape[1]
    tk = tq
    qt = qt_ref[...]
    row = lax.broadcasted_iota(jnp.int32, qt.shape, 0)
    zero = jnp.zeros_like(qt)
    q_sub = (jnp.where(row < DA_DH, qt, zero), jnp.where(row >= DA_DH, qt, zero))

    m_ref[...] = jnp.full(m_ref.shape, NEG, F32)
    l_ref[...] = jnp.zeros(l_ref.shape, F32)
    acc_ref[...] = jnp.zeros(acc_ref.shape, F32)

    def tile(kt, mask):
        start = pl.multiple_of(kt * tk, tk)
        kblk = k_ref[pl.ds(start, tk), :]
        vblk = vt_ref[:, pl.ds(start, tk)]
        for t in range(2):
            s = _dot(kblk, q_sub[t])
            if mask is not None:
                s = jnp.where(mask, s, NEG)
            m_old = m_ref[t]
            m_new = jnp.maximum(m_old, jnp.max(s, axis=0, keepdims=True))
            alpha = jnp.exp(m_old - m_new)
            p = jnp.exp(s - m_new)
            l_ref[t] = alpha * l_ref[t] + jnp.sum(p, axis=0, keepdims=True)
            acc_ref[t] = alpha * acc_ref[t] + _dot(vblk, p.astype(BF16))
            m_ref[t] = m_new

    def body(kt, carry):
        tile(kt, None)
        return carry

    lax.fori_loop(0, qi, body, 0)
    kc = lax.broadcasted_iota(jnp.int32, (tk, tq), 0) // CHUNK
    qc = lax.broadcasted_iota(jnp.int32, (tk, tq), 1) // CHUNK
    tile(qi, kc <= qc)

    lam = (jnp.exp(jnp.sum(lq1_ref[...] * lk1_ref[...], axis=1, keepdims=True))
           - jnp.exp(jnp.sum(lq2_ref[...] * lk2_ref[...], axis=1, keepdims=True)) + lam_init)
    o = acc_ref[0] / l_ref[0] - lam * (acc_ref[1] / l_ref[1])
    ms = jnp.mean(o * o, axis=0, keepdims=True)
    o = o * lax.rsqrt(ms + LN_EPS) * g_ref[...] * (1.0 - lam_init)
    o_ref[...] = o.T.astype(BF16)


def _diffattn(qt, k3, vt, lq1, lk1, lq2, lk2, g_col, lam_init):
    B, H, hw, S = qt.shape
    tq = ATT_TILE
    vec = _const_spec((1, DA_DH))
    return pl.pallas_call(
        functools.partial(_diffattn_kernel, lam_init=lam_init),
        grid=(B, H, S // tq),
        in_specs=[pl.BlockSpec((None, None, hw, tq), lambda b, h, q: (b, h, 0, q)),
                  pl.BlockSpec((None, S, hw), lambda b, h, q: (b, 0, h)),
                  pl.BlockSpec((None, None, hw, S), lambda b, h, q: (b, h, 0, 0)),
                  vec, vec, vec, vec, _const_spec((hw, 1))],
        out_specs=pl.BlockSpec((None, tq, hw), lambda b, h, q: (b, q, h)),
        out_shape=jax.ShapeDtypeStruct((B, S, H * hw), BF16),
        scratch_shapes=[pltpu.VMEM((2, 1, tq), F32), pltpu.VMEM((2, 1, tq), F32),
                        pltpu.VMEM((2, hw, tq), F32)],
        compiler_params=_params(3),
        name="diffattn",
    )(qt, k3, vt, lq1, lk1, lq2, lk2, g_col)


def _bias_kernel(f_ref, e_ref):
    tq, ext = e_ref.shape
    width = f_ref.shape[-1]
    g = jnp.broadcast_to(f_ref[...], (tq, width))
    g = pltpu.roll(g, width - REL_CLIP, axis=1, stride=1, stride_axis=0)
    bias = g[:, :ext]
    r = lax.broadcasted_iota(jnp.int32, (tq, ext), 0)
    u = lax.broadcasted_iota(jnp.int32, (tq, ext), 1)
    back = r // CHUNK + CB_LEFT - u // CHUNK
    e_ref[...] = jnp.where((back >= 0) & (back <= CB_LEFT), bias, NEG)


def _bias_table(f_ext):
    H, width = f_ext.shape
    return pl.pallas_call(
        _bias_kernel,
        grid=(H,),
        in_specs=[pl.BlockSpec((None, 1, width), lambda h: (h, 0, 0))],
        out_specs=pl.BlockSpec((None, BAND_TQ, BAND_EXT), lambda h: (h, 0, 0)),
        out_shape=jax.ShapeDtypeStruct((H, BAND_TQ, BAND_EXT), F32),
        compiler_params=_params(1),
        name="bias_table",
    )(f_ext.reshape(H, 1, width))


def _bandattn_kernel(q_ref, k_ref, v_ref, e_ref, o_ref):
    qi = pl.program_id(2)
    tq = q_ref.shape[0]
    lead = CB_LEFT * CHUNK
    want = qi * tq - lead
    start = pl.multiple_of(jnp.maximum(want, 0), tq)
    delta = pl.multiple_of(start - want, tq)
    kw = k_ref[pl.ds(start, BAND_WIN), :]
    vw = v_ref[pl.ds(start, BAND_WIN), :]
    q = q_ref[...]
    lane = lax.broadcasted_iota(jnp.int32, q.shape, 1)
    zero = jnp.zeros_like(q)
    outs = []
    for hh in range(2):
        sel = (lane < CB_DH) if hh == 0 else (lane >= CB_DH)
        s = lax.dot_general(jnp.where(sel, q, zero), kw, (((1,), (1,)), ((), ())),
                            preferred_element_type=F32)
        s = s + e_ref[hh, :, pl.ds(delta, BAND_WIN)]
        m = jnp.max(s, axis=1, keepdims=True)
        p = jnp.exp(s - m)
        l = jnp.sum(p, axis=1, keepdims=True)
        outs.append(_dot(p.astype(BF16), vw) / l)
    o_ref[...] = jnp.where(lane < CB_DH, outs[0], outs[1]).astype(BF16)


def _bandattn(qb, kb, vb, e_tab):
    B, S, W = qb.shape
    pairs = W // LANES
    tq = BAND_TQ
    return pl.pallas_call(
        _bandattn_kernel,
        grid=(B, pairs, S // tq),
        in_specs=[pl.BlockSpec((None, tq, LANES), lambda b, h, q: (b, q, h)),
                  pl.BlockSpec((None, S, LANES), lambda b, h, q: (b, 0, h)),
                  pl.BlockSpec((None, S, LANES), lambda b, h, q: (b, 0, h)),
                  pl.BlockSpec((2, tq, BAND_EXT), lambda b, h, q: (h, 0, 0))],
        out_specs=pl.BlockSpec((None, tq, LANES), lambda b, h, q: (b, q, h)),
        out_shape=jax.ShapeDtypeStruct((B, S, W), BF16),
        compiler_params=_params(3),
        name="bandattn",
    )(qb, kb, vb, e_tab)


def _layer_norm(z, g, b):
    mu = jnp.mean(z, axis=-1, keepdims=True)
    zc = z - mu
    var = jnp.mean(zc * zc, axis=-1, keepdims=True)
    return zc * lax.rsqrt(var + LN_EPS) * g + b


def _mixer_kernel(x_ref, oa_ref, ob_ref, wg_ref, bg_ref, wpa_ref, wpb_ref, wo_ref,
                  g_ref, b_ref, h_ref, *, alpha):
    xf = x_ref[...]
    d = xf.shape[1]
    gates = _dot(xf.astype(BF16), wg_ref[...]) + bg_ref[...]
    gates = 1.0 / (1.0 + jnp.exp(-gates))
    ya = _dot(oa_ref[...], wpa_ref[...])
    yb = _dot(ob_ref[...], wpb_ref[...])
    merged = gates[:, :d] * ya + gates[:, d:] * yb
    mix = _dot(merged.astype(BF16), wo_ref[...])
    h_ref[...] = _layer_norm(alpha * xf + mix, g_ref[...], b_ref[...])


def _mixer(x2, oa, ob, wg, bg, wpa, wpb, wo, g, b, alpha):
    T, D = x2.shape
    tm = ROW_TILE
    row = lambda w: pl.BlockSpec((tm, w), lambda i: (i, 0))
    return pl.pallas_call(
        functools.partial(_mixer_kernel, alpha=alpha),
        grid=(T // tm,),
        in_specs=[row(D), row(oa.shape[1]), row(ob.shape[1]), _const_spec(wg.shape),
                  _const_spec(bg.shape), _const_spec(wpa.shape), _const_spec(wpb.shape),
                  _const_spec(wo.shape), _const_spec(g.shape), _const_spec(b.shape)],
        out_specs=row(D),
        out_shape=jax.ShapeDtypeStruct((T, D), F32),
        compiler_params=_params(1),
        name="mixer",
    )(x2, oa, ob, wg, bg, wpa, wpb, wo, g, b)


def _convffn_kernel(h_ref, halo_ref, wu_ref, cw_ref, cb_ref, wd_ref, g_ref, b_ref,
                    o_ref, u_ref, a_ref, *, alpha, tiles_per_seq):
    i = pl.program_id(0)
    tm = h_ref.shape[0]
    dff = wd_ref.shape[0]
    hf = h_ref[...]
    hb = hf.astype(BF16)
    keep = jnp.where(i % tiles_per_seq == 0, 0.0, 1.0)
    pb = (halo_ref[...] * keep).astype(BF16)
    for c in range(dff // FF_CHUNK):
        conv = []
        for part in range(2):
            c0 = part * dff + c * FF_CHUNK
            w = wu_ref[:, c0:c0 + FF_CHUNK]
            u_ref[0:HALO, :] = _dot(pb, w)
            u_ref[HALO:, :] = _dot(hb, w)
            acc = cb_ref[:, c0:c0 + FF_CHUNK]
            for tap in range(CONV_W):
                lo = HALO - (CONV_W - 1) + tap
                acc = acc + u_ref[lo:lo + tm, :] * cw_ref[tap:tap + 1, c0:c0 + FF_CHUNK]
            conv.append(acc)
        gate, val = conv
        act = gate * (1.0 / (1.0 + jnp.exp(-gate))) * val
        a_ref[:, c * FF_CHUNK:(c + 1) * FF_CHUNK] = act.astype(BF16)
    f = _dot(a_ref[...], wd_ref[...])
    o_ref[...] = _layer_norm(alpha * hf + f, g_ref[...], b_ref[...])


def _convffn(h1, wu, cw, cb, wd, g, b, alpha, S):
    T, D = h1.shape
    tm = ROW_TILE
    dff = wd.shape[0]
    per_tile = tm // HALO
    return pl.pallas_call(
        functools.partial(_convffn_kernel, alpha=alpha, tiles_per_seq=S // tm),
        grid=(T // tm,),
        in_specs=[pl.BlockSpec((tm, D), lambda i: (i, 0)),
                  pl.BlockSpec((HALO, D), lambda i: (jnp.maximum(i * per_tile - 1, 0), 0)),
                  _const_spec(wu.shape), _const_spec(cw.shape), _const_spec(cb.shape),
                  _const_spec(wd.shape), _const_spec(g.shape), _const_spec(b.shape)],
        out_specs=pl.BlockSpec((tm, D), lambda i: (i, 0)),
        out_shape=jax.ShapeDtypeStruct((T, D), F32),
        scratch_shapes=[pltpu.VMEM((tm + HALO, FF_CHUNK), F32), pltpu.VMEM((tm, dff), BF16)],
        compiler_params=_params(1),
        name="convffn",
    )(h1, h1, wu, cw, cb, wd, g, b)


def kernel(x, positions, w_in, b_gate, lambda_q1, lambda_k1, lambda_q2, lambda_k2, subln_g,
           rel_bias, w_proj_a, w_proj_b, w_out, ln1_g, ln1_b, w_up, conv_w, conv_b, w_down,
           ln2_g, ln2_b):
    B, S, D = x.shape
    depth = w_in.shape[0]
    alpha = (2 * depth) ** 0.25
    attn_cols = 3 * DA_HEADS * DA_HW + 3 * CB_HEADS * CB_DH
    freq = 1.0 / (ROPE_THETA ** (jnp.arange(0, DA_DH, 2, dtype=F32) / DA_DH))
    inv = jnp.tile(freq, LANES // freq.shape[0]).reshape(1, LANES)
    pos2 = positions.reshape(B * S, 1)
    rel_size = rel_bias.shape[-1]
    f_width = BAND_EXT + BAND_TQ
    row2 = lambda v: v.reshape(1, -1)

    h = x.reshape(B * S, D)
    for l in range(depth):
        lam_init = 0.8 - 0.6 * math.exp(-0.3 * l)
        w_l = w_in[l].astype(BF16)
        qt, k2, vt, qb, kb, vb = _inproj(h, pos2, inv, w_l[:, :attn_cols], B, S)
        oa = _diffattn(qt, k2.reshape(B, S, -1), vt, row2(lambda_q1[l]), row2(lambda_k1[l]),
                       row2(lambda_q2[l]), row2(lambda_k2[l]), subln_g[l].reshape(-1, 1), lam_init)
        lead = CB_LEFT * CHUNK
        f_ext = jnp.pad(rel_bias[l], ((0, 0), (lead, f_width - lead - rel_size)), mode="edge")
        e_tab = _bias_table(f_ext)
        ob = _bandattn(qb.reshape(B, S, -1), kb.reshape(B, S, -1), vb.reshape(B, S, -1), e_tab)
        h = _mixer(h, oa.reshape(B * S, -1), ob.reshape(B * S, -1), w_l[:, attn_cols:],
                   row2(b_gate[l]), w_proj_a[l].astype(BF16), w_proj_b[l].astype(BF16),
                   w_out[l].astype(BF16), row2(ln1_g[l]), row2(ln1_b[l]), alpha)
        h = _convffn(h, w_up[l].astype(BF16), conv_w[l], row2(conv_b[l]), w_down[l].astype(BF16),
                     row2(ln2_g[l]), row2(ln2_b[l]), alpha, S)
    return h.reshape(B, S, D)
```

```python
import functools
import math

import jax
import jax.numpy as jnp
from jax import lax
from jax.experimental import pallas as pl
from jax.experimental.pallas import tpu as pltpu

F32 = jnp.float32
BF16 = jnp.bfloat16

LANES = 128
CHUNK = 64
DA_HEADS = 8
DA_DH = 64
DA_HW = 2 * DA_DH
CB_HEADS = 8
CB_DH = 64
CB_LEFT = 8
REL_CLIP = 256
ROPE_THETA = 10000.0
LN_EPS = 1e-5
CONV_W = 3
NEG = -1e30

VMEM_LIMIT = 56 * 1024 * 1024

ROW_TILE = 512
ATT_TILE = 512
BAND_TQ = 256
BAND_WIN = BAND_TQ + CB_LEFT * CHUNK
BAND_EXT = BAND_WIN + CB_LEFT * CHUNK
FF_CHUNK = 256
HALO = 8


def _dot(a, b):
    return jnp.dot(a, b, preferred_element_type=F32)


def _const_spec(shape):
    nd = len(shape)
    return pl.BlockSpec(shape, lambda *_: (0,) * nd, pipeline_mode=pl.Buffered(1))


def _params(n_axes):
    return pltpu.CompilerParams(dimension_semantics=("arbitrary",) * n_axes,
                                vmem_limit_bytes=VMEM_LIMIT)


def _inproj_kernel(x_ref, pos_ref, inv_ref, w_ref,
                   qt_ref, k_ref, vt_ref, qb_ref, kb_ref, vb_ref, *, scale_a, scale_b):
    tm = x_ref.shape[0]
    xb = x_ref[...].astype(BF16)
    ang = pos_ref[...].astype(F32) * inv_ref[...]
    cos = jnp.cos(ang)
    sin = jnp.sin(ang)
    lane = lax.broadcasted_iota(jnp.int32, (tm, LANES), 1)
    first_half = (lane % DA_DH) < (DA_DH // 2)
    sin = jnp.where(first_half, -sin, sin)

    def rope(t):
        partner = jnp.where(first_half,
                            pltpu.roll(t, LANES - DA_DH // 2, axis=1),
                            pltpu.roll(t, DA_DH // 2, axis=1))
        return t * cos + partner * sin

    grp = 4 * DA_HW
    qk_w = DA_HEADS * DA_HW
    for g in range(qk_w // grp):
        t = _dot(xb, w_ref[:, g * grp:(g + 1) * grp])
        for j in range(4):
            r = rope(t[:, j * DA_HW:(j + 1) * DA_HW]) * scale_a
            qt_ref[0, g * 4 + j] = r.T.astype(BF16)
    for g in range(qk_w // grp):
        t = _dot(xb, w_ref[:, qk_w + g * grp: qk_w + (g + 1) * grp])
        for j in range(4):
            c0 = g * grp + j * DA_HW
            k_ref[:, c0:c0 + DA_HW] = rope(t[:, j * DA_HW:(j + 1) * DA_HW]).astype(BF16)
    for g in range(qk_w // grp):
        t = _dot(xb, w_ref[:, 2 * qk_w + g * grp: 2 * qk_w + (g + 1) * grp])
        for j in range(4):
            vt_ref[0, g * 4 + j] = t[:, j * DA_HW:(j + 1) * DA_HW].T.astype(BF16)
    off = 3 * qk_w
    cbw = CB_HEADS * CB_DH
    qb_ref[...] = (_dot(xb, w_ref[:, off:off + cbw]) * scale_b).astype(BF16)
    kb_ref[...] = _dot(xb, w_ref[:, off + cbw:off + 2 * cbw]).astype(BF16)
    vb_ref[...] = _dot(xb, w_ref[:, off + 2 * cbw:off + 3 * cbw]).astype(BF16)


def _inproj(x2, pos2, inv, w_attn, B, S):
    T, D = x2.shape
    tm = ROW_TILE
    spb = S // tm
    qk_w = DA_HEADS * DA_HW
    cbw = CB_HEADS * CB_DH
    t_spec = pl.BlockSpec((1, DA_HEADS, DA_HW, tm), lambda i: (i // spb, 0, 0, i % spb))
    row = lambda w: pl.BlockSpec((tm, w), lambda i: (i, 0))
    t_shape = jax.ShapeDtypeStruct((B, DA_HEADS, DA_HW, S), BF16)
    return pl.pallas_call(
        functools.partial(_inproj_kernel, scale_a=DA_DH ** -0.5 * math.log2(math.e), scale_b=CB_DH ** -0.5),
        grid=(T // tm,),
        in_specs=[row(D), row(1), _const_spec((1, LANES)), _const_spec(w_attn.shape)],
        out_specs=[t_spec, row(qk_w), t_spec, row(cbw), row(cbw), row(cbw)],
        out_shape=[t_shape, jax.ShapeDtypeStruct((T, qk_w), BF16), t_shape,
                   jax.ShapeDtypeStruct((T, cbw), BF16), jax.ShapeDtypeStruct((T, cbw), BF16),
                   jax.ShapeDtypeStruct((T, cbw), BF16)],
        compiler_params=_params(1),
        name="inproj",
    )(x2, pos2, inv, w_attn)


def _diffattn_kernel(qt_ref, k_ref, vt_ref, lq1_ref, lk1_ref, lq2_ref, lk2_ref, g_ref,
                     o_ref, m_ref, l_ref, acc_ref, sa_ref, sb_ref, *, lam_init):
    qi = pl.program_id(2)
    tq = qt_ref.shape[1]
    tk = tq
    qt = qt_ref[...]
    row = lax.broadcasted_iota(jnp.int32, qt.shape, 0)
    zero = jnp.zeros_like(qt)
    q_sub = (jnp.where(row < DA_DH, qt, zero), jnp.where(row >= DA_DH, qt, zero))

    m_ref[...] = jnp.full(m_ref.shape, NEG, F32)
    l_ref[...] = jnp.zeros(l_ref.shape, F32)
    acc_ref[...] = jnp.zeros(acc_ref.shape, F32)

    def scores(kt, s_ref):
        kblk = k_ref[pl.ds(pl.multiple_of(kt * tk, tk), tk), :]
        for t in range(2):
            s_ref[t] = _dot(kblk, q_sub[t])

    def consume(kt, s_ref, mask):
        vblk = vt_ref[:, pl.ds(pl.multiple_of(kt * tk, tk), tk)]
        for t in range(2):
            s = s_ref[t]
            if mask is not None:
                s = jnp.where(mask, s, NEG)
            m_old = m_ref[t]
            m_new = jnp.maximum(m_old, jnp.max(s, axis=0, keepdims=True))
            alpha = jnp.exp2(m_old - m_new)
            p = jnp.exp2(s - m_new)
            l_ref[t] = alpha * l_ref[t] + jnp.sum(p, axis=0, keepdims=True)
            acc_ref[t] = alpha * acc_ref[t] + _dot(vblk, p.astype(BF16))
            m_ref[t] = m_new

    scores(0, sa_ref)

    def pair(j, carry):
        kt = 2 * j
        scores(kt + 1, sb_ref)
        consume(kt, sa_ref, None)
        scores(kt + 2, sa_ref)
        consume(kt + 1, sb_ref, None)
        return carry

    lax.fori_loop(0, qi // 2, pair, 0)
    kc = lax.broadcasted_iota(jnp.int32, (tk, tq), 0) // CHUNK
    qc = lax.broadcasted_iota(jnp.int32, (tk, tq), 1) // CHUNK
    diag = kc <= qc

    @pl.when(qi % 2 == 0)
    def _():
        consume(qi, sa_ref, diag)

    @pl.when(qi % 2 == 1)
    def _():
        scores(qi, sb_ref)
        consume(qi - 1, sa_ref, None)
        consume(qi, sb_ref, diag)

    lam = (jnp.exp(jnp.sum(lq1_ref[...] * lk1_ref[...], axis=1, keepdims=True))
           - jnp.exp(jnp.sum(lq2_ref[...] * lk2_ref[...], axis=1, keepdims=True)) + lam_init)
    o = acc_ref[0] / l_ref[0] - lam * (acc_ref[1] / l_ref[1])
    ms = jnp.mean(o * o, axis=0, keepdims=True)
    o = o * lax.rsqrt(ms + LN_EPS) * g_ref[...] * (1.0 - lam_init)
    o_ref[...] = o.T.astype(BF16)


def _diffattn(qt, k3, vt, lq1, lk1, lq2, lk2, g_col, lam_init):
    B, H, hw, S = qt.shape
    tq = ATT_TILE
    vec = _const_spec((1, DA_DH))
    return pl.pallas_call(
        functools.partial(_diffattn_kernel, lam_init=lam_init),
        grid=(B, H, S // tq),
        in_specs=[pl.BlockSpec((None, None, hw, tq), lambda b, h, q: (b, h, 0, q)),
                  pl.BlockSpec((None, S, hw), lambda b, h, q: (b, 0, h)),
                  pl.BlockSpec((None, None, hw, S), lambda b, h, q: (b, h, 0, 0)),
                  vec, vec, vec, vec, _const_spec((hw, 1))],
        out_specs=pl.BlockSpec((None, tq, hw), lambda b, h, q: (b, q, h)),
        out_shape=jax.ShapeDtypeStruct((B, S, H * hw), BF16),
        scratch_shapes=[pltpu.VMEM((2, 1, tq), F32), pltpu.VMEM((2, 1, tq), F32),
                        pltpu.VMEM((2, hw, tq), F32),
                        pltpu.VMEM((2, tq, tq), F32), pltpu.VMEM((2, tq, tq), F32)],
        compiler_params=_params(3),
        name="diffattn",
    )(qt, k3, vt, lq1, lk1, lq2, lk2, g_col)


def _bias_kernel(f_ref, e_ref):
    tq, ext = e_ref.shape
    width = f_ref.shape[-1]
    g = jnp.broadcast_to(f_ref[...], (tq, width))
    g = pltpu.roll(g, width - REL_CLIP, axis=1, stride=1, stride_axis=0)
    bias = g[:, :ext]
    r = lax.broadcasted_iota(jnp.int32, (tq, ext), 0)
    u = lax.broadcasted_iota(jnp.int32, (tq, ext), 1)
    back = r // CHUNK + CB_LEFT - u // CHUNK
    e_ref[...] = jnp.where((back >= 0) & (back <= CB_LEFT), bias, NEG)


def _bias_table(f_ext):
    H, width = f_ext.shape
    return pl.pallas_call(
        _bias_kernel,
        grid=(H,),
        in_specs=[pl.BlockSpec((None, 1, width), lambda h: (h, 0, 0))],
        out_specs=pl.BlockSpec((None, BAND_TQ, BAND_EXT), lambda h: (h, 0, 0)),
        out_shape=jax.ShapeDtypeStruct((H, BAND_TQ, BAND_EXT), F32),
        compiler_params=_params(1),
        name="bias_table",
    )(f_ext.reshape(H, 1, width))


def _bandattn_kernel(q_ref, k_ref, v_ref, e_ref, o_ref):
    qi = pl.program_id(2)
    tq = q_ref.shape[0]
    lead = CB_LEFT * CHUNK
    want = qi * tq - lead
    start = pl.multiple_of(jnp.maximum(want, 0), tq)
    delta = pl.multiple_of(start - want, tq)
    kw = k_ref[pl.ds(start, BAND_WIN), :]
    vw = v_ref[pl.ds(start, BAND_WIN), :]
    q = q_ref[...]
    lane = lax.broadcasted_iota(jnp.int32, q.shape, 1)
    zero = jnp.zeros_like(q)
    outs = []
    for hh in range(2):
        sel = (lane < CB_DH) if hh == 0 else (lane >= CB_DH)
        s = lax.dot_general(jnp.where(sel, q, zero), kw, (((1,), (1,)), ((), ())),
                            preferred_element_type=F32)
        s = s + e_ref[hh, :, pl.ds(delta, BAND_WIN)]
        m = jnp.max(s, axis=1, keepdims=True)
        p = jnp.exp(s - m)
        l = jnp.sum(p, axis=1, keepdims=True)
        outs.append(_dot(p.astype(BF16), vw) / l)
    o_ref[...] = jnp.where(lane < CB_DH, outs[0], outs[1]).astype(BF16)


def _bandattn(qb, kb, vb, e_tab):
    B, S, W = qb.shape
    pairs = W // LANES
    tq = BAND_TQ
    return pl.pallas_call(
        _bandattn_kernel,
        grid=(B, pairs, S // tq),
        in_specs=[pl.BlockSpec((None, tq, LANES), lambda b, h, q: (b, q, h)),
                  pl.BlockSpec((None, S, LANES), lambda b, h, q: (b, 0, h)),
                  pl.BlockSpec((None, S, LANES), lambda b, h, q: (b, 0, h)),
                  pl.BlockSpec((2, tq, BAND_EXT), lambda b, h, q: (h, 0, 0))],
        out_specs=pl.BlockSpec((None, tq, LANES), lambda b, h, q: (b, q, h)),
        out_shape=jax.ShapeDtypeStruct((B, S, W), BF16),
        compiler_params=_params(3),
        name="bandattn",
    )(qb, kb, vb, e_tab)


def _layer_norm(z, g, b):
    mu = jnp.mean(z, axis=-1, keepdims=True)
    zc = z - mu
    var = jnp.mean(zc * zc, axis=-1, keepdims=True)
    return zc * lax.rsqrt(var + LN_EPS) * g + b


def _mixer_kernel(x_ref, oa_ref, ob_ref, wg_ref, bg_ref, wpa_ref, wpb_ref, wo_ref,
                  g_ref, b_ref, h_ref, *, alpha):
    xf = x_ref[...]
    d = xf.shape[1]
    gates = _dot(xf.astype(BF16), wg_ref[...]) + bg_ref[...]
    gates = 1.0 / (1.0 + jnp.exp(-gates))
    ya = _dot(oa_ref[...], wpa_ref[...])
    yb = _dot(ob_ref[...], wpb_ref[...])
    merged = gates[:, :d] * ya + gates[:, d:] * yb
    mix = _dot(merged.astype(BF16), wo_ref[...])
    h_ref[...] = _layer_norm(alpha * xf + mix, g_ref[...], b_ref[...])


def _mixer(x2, oa, ob, wg, bg, wpa, wpb, wo, g, b, alpha):
    T, D = x2.shape
    tm = ROW_TILE
    row = lambda w: pl.BlockSpec((tm, w), lambda i: (i, 0))
    return pl.pallas_call(
        functools.partial(_mixer_kernel, alpha=alpha),
        grid=(T // tm,),
        in_specs=[row(D), row(oa.shape[1]), row(ob.shape[1]), _const_spec(wg.shape),
                  _const_spec(bg.shape), _const_spec(wpa.shape), _const_spec(wpb.shape),
                  _const_spec(wo.shape), _const_spec(g.shape), _const_spec(b.shape)],
        out_specs=row(D),
        out_shape=jax.ShapeDtypeStruct((T, D), F32),
        compiler_params=_params(1),
        name="mixer",
    )(x2, oa, ob, wg, bg, wpa, wpb, wo, g, b)


def _convffn_kernel(h_ref, halo_ref, wu_ref, cw_ref, cb_ref, wd_ref, g_ref, b_ref,
                    o_ref, u_ref, a_ref, *, alpha, tiles_per_seq):
    i = pl.program_id(0)
    tm = h_ref.shape[0]
    dff = wd_ref.shape[0]
    hf = h_ref[...]
    hb = hf.astype(BF16)
    keep = jnp.where(i % tiles_per_seq == 0, 0.0, 1.0)
    pb = (halo_ref[...] * keep).astype(BF16)
    for c in range(dff // FF_CHUNK):
        conv = []
        for part in range(2):
            c0 = part * dff + c * FF_CHUNK
            w = wu_ref[:, c0:c0 + FF_CHUNK]
            u_ref[0:HALO, :] = _dot(pb, w)
            u_ref[HALO:, :] = _dot(hb, w)
            acc = cb_ref[:, c0:c0 + FF_CHUNK]
            for tap in range(CONV_W):
                lo = HALO - (CONV_W - 1) + tap
                acc = acc + u_ref[lo:lo + tm, :] * cw_ref[tap:tap + 1, c0:c0 + FF_CHUNK]
            conv.append(acc)
        gate, val = conv
        act = gate * (1.0 / (1.0 + jnp.exp(-gate))) * val
        a_ref[:, c * FF_CHUNK:(c + 1) * FF_CHUNK] = act.astype(BF16)
    f = _dot(a_ref[...], wd_ref[...])
    o_ref[...] = _layer_norm(alpha * hf + f, g_ref[...], b_ref[...])


def _convffn(h1, wu, cw, cb, wd, g, b, alpha, S):
    T, D = h1.shape
    tm = ROW_TILE
    dff = wd.shape[0]
    per_tile = tm // HALO
    return pl.pallas_call(
        functools.partial(_convffn_kernel, alpha=alpha, tiles_per_seq=S // tm),
        grid=(T // tm,),
        in_specs=[pl.BlockSpec((tm, D), lambda i: (i, 0)),
                  pl.BlockSpec((HALO, D), lambda i: (jnp.maximum(i * per_tile - 1, 0), 0)),
                  _const_spec(wu.shape), _const_spec(cw.shape), _const_spec(cb.shape),
                  _const_spec(wd.shape), _const_spec(g.shape), _const_spec(b.shape)],
        out_specs=pl.BlockSpec((tm, D), lambda i: (i, 0)),
        out_shape=jax.ShapeDtypeStruct((T, D), F32),
        scratch_shapes=[pltpu.VMEM((tm + HALO, FF_CHUNK), F32), pltpu.VMEM((tm, dff), BF16)],
        compiler_params=_params(1),
        name="convffn",
    )(h1, h1, wu, cw, cb, wd, g, b)


def kernel(x, positions, w_in, b_gate, lambda_q1, lambda_k1, lambda_q2, lambda_k2, subln_g,
           rel_bias, w_proj_a, w_proj_b, w_out, ln1_g, ln1_b, w_up, conv_w, conv_b, w_down,
           ln2_g, ln2_b):
    B, S, D = x.shape
    depth = w_in.shape[0]
    alpha = (2 * depth) ** 0.25
    attn_cols = 3 * DA_HEADS * DA_HW + 3 * CB_HEADS * CB_DH
    freq = 1.0 / (ROPE_THETA ** (jnp.arange(0, DA_DH, 2, dtype=F32) / DA_DH))
    inv = jnp.tile(freq, LANES // freq.shape[0]).reshape(1, LANES)
    pos2 = positions.reshape(B * S, 1)
    rel_size = rel_bias.shape[-1]
    f_width = BAND_EXT + BAND_TQ
    row2 = lambda v: v.reshape(1, -1)

    h = x.reshape(B * S, D)
    for l in range(depth):
        lam_init = 0.8 - 0.6 * math.exp(-0.3 * l)
        w_l = w_in[l].astype(BF16)
        qt, k2, vt, qb, kb, vb = _inproj(h, pos2, inv, w_l[:, :attn_cols], B, S)
        oa = _diffattn(qt, k2.reshape(B, S, -1), vt, row2(lambda_q1[l]), row2(lambda_k1[l]),
                       row2(lambda_q2[l]), row2(lambda_k2[l]), subln_g[l].reshape(-1, 1), lam_init)
        lead = CB_LEFT * CHUNK
        f_ext = jnp.pad(rel_bias[l], ((0, 0), (lead, f_width - lead - rel_size)), mode="edge")
        e_tab = _bias_table(f_ext)
        ob = _bandattn(qb.reshape(B, S, -1), kb.reshape(B, S, -1), vb.reshape(B, S, -1), e_tab)
        h = _mixer(h, oa.reshape(B * S, -1), ob.reshape(B * S, -1), w_l[:, attn_cols:],
                   row2(b_gate[l]), w_proj_a[l].astype(BF16), w_proj_b[l].astype(BF16),
                   w_out[l].astype(BF16), row2(ln1_g[l]), row2(ln1_b[l]), alpha)
        h = _convffn(h, w_up[l].astype(BF16), conv_w[l], row2(conv_b[l]), w_down[l].astype(BF16),
                     row2(ln2_g[l]), row2(ln2_b[l]), alpha, S)
    return h.reshape(B, S, D)
```

```python
import functools
import math

import jax
import jax.numpy as jnp
from jax import lax
from jax.experimental import pallas as pl
from jax.experimental.pallas import tpu as pltpu

F32 = jnp.float32
BF16 = jnp.bfloat16

LANES = 128
BF16_ROWS = 16
CHUNK = 64
DA_HEADS = 8
DA_DH = 64
DA_HW = 2 * DA_DH
CB_HEADS = 8
CB_DH = 64
CB_LEFT = 8
REL_CLIP = 256
ROPE_THETA = 10000.0
LN_EPS = 1e-5
CONV_W = 3
NEG = -1e30
LOG2E = math.log2(math.e)

VMEM_LIMIT = 56 * 1024 * 1024

ROW_TILE = 512
ATT_TILE = 512
ATT_UNROLL = 2
VT_ROWS = LANES + BF16_ROWS
BAND_TQ = 256
BAND_GROUP = 4
BAND_WIN = BAND_TQ + CB_LEFT * CHUNK
BAND_EXT = BAND_WIN + CB_LEFT * CHUNK
FF_CHUNK = 256
HALO = BF16_ROWS


def _dot(a, b):
    return jnp.dot(a, b, preferred_element_type=F32)


def _const_spec(shape):
    nd = len(shape)
    return pl.BlockSpec(shape, lambda *_: (0,) * nd, pipeline_mode=pl.Buffered(1))


def _params(n_axes, flags=None):
    return pltpu.CompilerParams(dimension_semantics=("arbitrary",) * n_axes,
                                vmem_limit_bytes=VMEM_LIMIT, flags=flags)


def _inproj_kernel(x_ref, pos_ref, inv_ref, w_ref,
                   qt_ref, k_ref, vt_ref, qbt_ref, kb_ref, vbt_ref, *, scale_a, scale_b):
    tm = x_ref.shape[0]
    xb = x_ref[...].astype(BF16)
    ang = pos_ref[...].astype(F32) * inv_ref[...]
    cos = jnp.cos(ang)
    sin = jnp.sin(ang)
    lane = lax.broadcasted_iota(jnp.int32, (tm, LANES), 1)
    first_half = (lane % DA_DH) < (DA_DH // 2)
    sin = jnp.where(first_half, -sin, sin)
    ones = jnp.ones((BF16_ROWS, tm), BF16)

    def rope(t):
        partner = jnp.where(first_half,
                            pltpu.roll(t, LANES - DA_DH // 2, axis=1),
                            pltpu.roll(t, DA_DH // 2, axis=1))
        return t * cos + partner * sin

    grp = 4 * LANES
    qk_w = DA_HEADS * DA_HW
    for g in range(qk_w // grp):
        t = _dot(xb, w_ref[:, g * grp:(g + 1) * grp])
        for j in range(4):
            r = rope(t[:, j * LANES:(j + 1) * LANES]) * scale_a
            qt_ref[0, g * 4 + j] = r.T.astype(BF16)
    for g in range(qk_w // grp):
        t = _dot(xb, w_ref[:, qk_w + g * grp: qk_w + (g + 1) * grp])
        for j in range(4):
            c0 = g * grp + j * LANES
            k_ref[:, c0:c0 + LANES] = rope(t[:, j * LANES:(j + 1) * LANES]).astype(BF16)
    for g in range(qk_w // grp):
        t = _dot(xb, w_ref[:, 2 * qk_w + g * grp: 2 * qk_w + (g + 1) * grp])
        for j in range(4):
            vt_ref[0, g * 4 + j, 0:LANES, :] = t[:, j * LANES:(j + 1) * LANES].T.astype(BF16)
            vt_ref[0, g * 4 + j, LANES:VT_ROWS, :] = ones
    off = 3 * qk_w
    cbw = CB_HEADS * CB_DH
    t = _dot(xb, w_ref[:, off:off + cbw]) * scale_b
    for j in range(cbw // LANES):
        qbt_ref[0, j] = t[:, j * LANES:(j + 1) * LANES].T.astype(BF16)
    kb_ref[...] = _dot(xb, w_ref[:, off + cbw:off + 2 * cbw]).astype(BF16)
    t = _dot(xb, w_ref[:, off + 2 * cbw:off + 3 * cbw])
    for j in range(cbw // LANES):
        vbt_ref[0, j, 0:LANES, :] = t[:, j * LANES:(j + 1) * LANES].T.astype(BF16)
        vbt_ref[0, j, LANES:VT_ROWS, :] = ones


def _inproj(x2, pos2, inv, w_attn, B, S):
    T, D = x2.shape
    tm = ROW_TILE
    spb = S // tm
    qk_w = DA_HEADS * DA_HW
    cbw = CB_HEADS * CB_DH
    pairs = cbw // LANES

    def t_spec(heads, rows):
        return pl.BlockSpec((1, heads, rows, tm), lambda i: (i // spb, 0, 0, i % spb))

    def t_shape(heads, rows):
        return jax.ShapeDtypeStruct((B, heads, rows, S), BF16)

    row = lambda w: pl.BlockSpec((tm, w), lambda i: (i, 0))
    return pl.pallas_call(
        functools.partial(_inproj_kernel, scale_a=DA_DH ** -0.5 * LOG2E,
                          scale_b=CB_DH ** -0.5 * LOG2E),
        grid=(T // tm,),
        in_specs=[row(D), row(1), _const_spec((1, LANES)), _const_spec(w_attn.shape)],
        out_specs=[t_spec(DA_HEADS, LANES), row(qk_w), t_spec(DA_HEADS, VT_ROWS),
                   t_spec(pairs, LANES), row(cbw), t_spec(pairs, VT_ROWS)],
        out_shape=[t_shape(DA_HEADS, LANES), jax.ShapeDtypeStruct((T, qk_w), BF16),
                   t_shape(DA_HEADS, VT_ROWS), t_shape(pairs, LANES),
                   jax.ShapeDtypeStruct((T, cbw), BF16), t_shape(pairs, VT_ROWS)],
        compiler_params=_params(1),
        name="inproj",
    )(x2, pos2, inv, w_attn)


def _diffattn_kernel(qt_ref, k_ref, vt_ref, lq1_ref, lk1_ref, lq2_ref, lk2_ref, g_ref,
                     o_ref, m_ref, acc_ref, s_ref, t_ref, a_ref, p_ref,
                     *, lam_init, tile, unroll):
    S = qt_ref.shape[1]
    tq = tk = tile
    nq = S // tq
    n_plain = nq * (nq - 1) // 2
    row = lax.broadcasted_iota(jnp.int32, (DA_HW, tq), 0)
    kc = lax.broadcasted_iota(jnp.int32, (tk, tq), 0) // CHUNK
    qc = lax.broadcasted_iota(jnp.int32, (tk, tq), 1) // CHUNK
    diag = kc <= qc

    def scores(tile_id, slot, masked, t):
        qi, kt = tile_id
        qt = qt_ref[:, pl.ds(pl.multiple_of(qi * tq, tq), tq)]
        kblk = k_ref[pl.ds(pl.multiple_of(kt * tk, tk), tk), :]
        sub = (row < DA_DH) if t == 0 else (row >= DA_DH)
        s = _dot(kblk, jnp.where(sub, qt, jnp.zeros_like(qt)))
        if masked:
            s = jnp.where(diag, s, NEG)
        s_ref[slot, t] = s
        t_ref[slot, t] = jnp.max(s, axis=0, keepdims=True)

    def softmax(tile_id, slot, t):
        qi, _ = tile_id
        m_old = m_ref[qi, t]
        m_new = jnp.maximum(m_old, t_ref[slot, t])
        a_ref[slot, t] = jnp.exp2(m_old - m_new)
        p_ref[slot, t] = jnp.exp2((s_ref[slot, t] - m_new).astype(BF16))
        m_ref[qi, t] = m_new

    def pv(tile_id, slot, t):
        qi, kt = tile_id
        vblk = vt_ref[:, pl.ds(pl.multiple_of(kt * tk, tk), tk)]
        acc_ref[qi, t] = (a_ref[slot, t] * acc_ref[qi, t]
                          + _dot(vblk, p_ref[slot, t]))

    def run(first, step, n, masked):
        assert n % 2 == 0 and (n - 2) % unroll == 0 and unroll % 2 == 0
        t0 = first
        t1 = step(t0)
        for t in range(2):
            scores(t0, 0, masked, t)
        for t in range(2):
            scores(t1, 1, masked, t)
            softmax(t0, 0, t)

        def body(_, carry):
            a, b = carry
            for u in range(unroll):
                c = step(b)
                for t in range(2):
                    scores(c, u % 2, masked, t)
                for t in range(2):
                    softmax(b, (u + 1) % 2, t)
                for t in range(2):
                    pv(a, u % 2, t)
                a, b = b, c
            return a, b

        a, b = lax.fori_loop(0, (n - 2) // unroll, body, (t0, t1))
        for t in range(2):
            softmax(b, 1, t)
            pv(a, 0, t)
        for t in range(2):
            pv(b, 1, t)

    m_ref[...] = jnp.full(m_ref.shape, NEG, F32)
    acc_ref[...] = jnp.zeros(acc_ref.shape, F32)

    def next_diag(tile_id):
        q = jnp.minimum(tile_id[0] + 1, nq - 1)
        return q, q

    def next_plain(tile_id):
        qi, kt = tile_id
        wrap = kt + 1 == qi
        return (jnp.where(wrap, jnp.minimum(qi + 1, nq - 1), qi), jnp.where(wrap, 0, kt + 1))

    run((jnp.int32(0), jnp.int32(0)), next_diag, nq, True)
    run((jnp.int32(1), jnp.int32(0)), next_plain, n_plain, False)

    lam = (jnp.exp(jnp.sum(lq1_ref[...] * lk1_ref[...], axis=1, keepdims=True))
           - jnp.exp(jnp.sum(lq2_ref[...] * lk2_ref[...], axis=1, keepdims=True)) + lam_init)

    def epilogue(qi, carry):
        a0 = acc_ref[qi, 0]
        a1 = acc_ref[qi, 1]
        o = (a0[:DA_HW] / a0[DA_HW:DA_HW + 1]
             - lam * (a1[:DA_HW] / a1[DA_HW:DA_HW + 1]))
        ms = jnp.mean(o * o, axis=0, keepdims=True)
        o = o * lax.rsqrt(ms + LN_EPS) * g_ref[...] * (1.0 - lam_init)
        o_ref[pl.ds(pl.multiple_of(qi * tq, tq), tq), :] = o.T.astype(BF16)
        return carry

    lax.fori_loop(0, nq, epilogue, 0)


def _diffattn(qt, k3, vt, lq1, lk1, lq2, lk2, g_col, lam_init):
    B, H, hw, S = qt.shape
    tq = ATT_TILE
    nq = S // tq
    vec = _const_spec((1, DA_DH))
    return pl.pallas_call(
        functools.partial(_diffattn_kernel, lam_init=lam_init, tile=tq, unroll=ATT_UNROLL),
        grid=(B, H),
        in_specs=[pl.BlockSpec((None, None, hw, S), lambda b, h: (b, h, 0, 0)),
                  pl.BlockSpec((None, S, hw), lambda b, h: (b, 0, h)),
                  pl.BlockSpec((None, None, VT_ROWS, S), lambda b, h: (b, h, 0, 0)),
                  vec, vec, vec, vec, _const_spec((hw, 1))],
        out_specs=pl.BlockSpec((None, S, hw), lambda b, h: (b, 0, h)),
        out_shape=jax.ShapeDtypeStruct((B, S, H * hw), BF16),
        scratch_shapes=[pltpu.VMEM((nq, 2, 1, tq), F32), pltpu.VMEM((nq, 2, VT_ROWS, tq), F32),
                        pltpu.VMEM((2, 2, tq, tq), F32), pltpu.VMEM((2, 2, 1, tq), F32),
                        pltpu.VMEM((2, 2, 1, tq), F32), pltpu.VMEM((2, 2, tq, tq), BF16)],
        compiler_params=_params(2),
        name="diffattn",
    )(qt, k3, vt, lq1, lk1, lq2, lk2, g_col)


def _bias_kernel(f_ref, e_ref):
    ext, tq = e_ref.shape
    width = f_ref.shape[-1]
    g = jnp.broadcast_to(f_ref[...], (tq, width))
    g = pltpu.roll(g, width - REL_CLIP, axis=1, stride=1, stride_axis=0)
    bias = g[:, :ext] * LOG2E
    r = lax.broadcasted_iota(jnp.int32, (tq, ext), 0)
    u = lax.broadcasted_iota(jnp.int32, (tq, ext), 1)
    back = r // CHUNK + CB_LEFT - u // CHUNK
    e_ref[...] = jnp.where((back >= 0) & (back <= CB_LEFT), bias, NEG).T


def _bias_table(f_ext):
    H, width = f_ext.shape
    return pl.pallas_call(
        _bias_kernel,
        grid=(H,),
        in_specs=[pl.BlockSpec((None, 1, width), lambda h: (h, 0, 0))],
        out_specs=pl.BlockSpec((None, BAND_EXT, BAND_TQ), lambda h: (h, 0, 0)),
        out_shape=jax.ShapeDtypeStruct((H, BAND_EXT, BAND_TQ), F32),
        compiler_params=_params(1),
        name="bias_table",
    )(f_ext.reshape(H, 1, width))


def _bandattn_kernel(qt_ref, k_ref, vt_ref, e_ref, o_ref):
    tq = BAND_TQ
    lead = CB_LEFT * CHUNK
    row = lax.broadcasted_iota(jnp.int32, (LANES, tq), 0)
    chains = [(g, hh) for g in range(BAND_GROUP) for hh in range(2)]

    def window(g):
        want = (pl.program_id(2) * BAND_GROUP + g) * tq - lead
        start = pl.multiple_of(jnp.maximum(want, 0), tq)
        return start, pl.multiple_of(start - want, tq)

    def scores(g, hh):
        start, delta = window(g)
        kw = k_ref[pl.ds(start, BAND_WIN), :]
        qt = qt_ref[:, g * tq:(g + 1) * tq]
        sel = (row < CB_DH) if hh == 0 else (row >= CB_DH)
        return (_dot(kw, jnp.where(sel, qt, jnp.zeros_like(qt)))
                + e_ref[hh, pl.ds(delta, BAND_WIN), :])

    def softmax(s):
        return jnp.exp2((s - jnp.max(s, axis=0, keepdims=True)).astype(BF16))

    def pv(g, p):
        start, _ = window(g)
        r = _dot(vt_ref[:, pl.ds(start, BAND_WIN)], p)
        return r[:LANES] / r[LANES:LANES + 1]

    n = len(chains)
    s, p, o = {}, {}, {}
    s[0] = scores(*chains[0])
    s[1] = scores(*chains[1])
    p[0] = softmax(s.pop(0))
    for i in range(n):
        if i + 2 < n:
            s[i + 2] = scores(*chains[i + 2])
        if i + 1 < n:
            p[i + 1] = softmax(s.pop(i + 1))
        g, hh = chains[i]
        o[hh] = pv(g, p.pop(i))
        if hh == 1:
            both = jnp.where(row < CB_DH, o[0], o[1])
            o_ref[g * tq:(g + 1) * tq, :] = both.T.astype(BF16)


def _bandattn(qbt, kb3, vbt, e_tab):
    B, pairs, _, S = qbt.shape
    tq = BAND_TQ * BAND_GROUP
    return pl.pallas_call(
        _bandattn_kernel,
        grid=(B, pairs, S // tq),
        in_specs=[pl.BlockSpec((None, None, LANES, tq), lambda b, h, q: (b, h, 0, q)),
                  pl.BlockSpec((None, S, LANES), lambda b, h, q: (b, 0, h)),
                  pl.BlockSpec((None, None, VT_ROWS, S), lambda b, h, q: (b, h, 0, 0)),
                  pl.BlockSpec((2, BAND_EXT, BAND_TQ), lambda b, h, q: (h, 0, 0))],
        out_specs=pl.BlockSpec((None, tq, LANES), lambda b, h, q: (b, q, h)),
        out_shape=jax.ShapeDtypeStruct((B, S, pairs * LANES), BF16),
        compiler_params=_params(3),
        name="bandattn",
    )(qbt, kb3, vbt, e_tab)


def _layer_norm(z, g, b):
    mu = jnp.mean(z, axis=-1, keepdims=True)
    zc = z - mu
    var = jnp.mean(zc * zc, axis=-1, keepdims=True)
    return zc * lax.rsqrt(var + LN_EPS) * g + b


def _mixer_kernel(x_ref, oa_ref, ob_ref, wg_ref, bg_ref, wpa_ref, wpb_ref, wo_ref,
                  g_ref, b_ref, h_ref, *, alpha):
    xf = x_ref[...]
    d = xf.shape[1]
    gates = _dot(xf.astype(BF16), wg_ref[...]) + bg_ref[...]
    gates = 1.0 / (1.0 + jnp.exp(-gates))
    ya = _dot(oa_ref[...], wpa_ref[...])
    yb = _dot(ob_ref[...], wpb_ref[...])
    merged = gates[:, :d] * ya + gates[:, d:] * yb
    mix = _dot(merged.astype(BF16), wo_ref[...])
    h_ref[...] = _layer_norm(alpha * xf + mix, g_ref[...], b_ref[...])


def _mixer(x2, oa, ob, wg, bg, wpa, wpb, wo, g, b, alpha):
    T, D = x2.shape
    tm = ROW_TILE
    row = lambda w: pl.BlockSpec((tm, w), lambda i: (i, 0))
    return pl.pallas_call(
        functools.partial(_mixer_kernel, alpha=alpha),
        grid=(T // tm,),
        in_specs=[row(D), row(oa.shape[1]), row(ob.shape[1]), _const_spec(wg.shape),
                  _const_spec(bg.shape), _const_spec(wpa.shape), _const_spec(wpb.shape),
                  _const_spec(wo.shape), _const_spec(g.shape), _const_spec(b.shape)],
        out_specs=row(D),
        out_shape=jax.ShapeDtypeStruct((T, D), F32),
        compiler_params=_params(1),
        name="mixer",
    )(x2, oa, ob, wg, bg, wpa, wpb, wo, g, b)


def _convffn_kernel(h_ref, halo_ref, wu_ref, cw_ref, cb_ref, wd_ref, g_ref, b_ref,
                    o_ref, hx_ref, u_ref, a_ref, *, alpha, tiles_per_seq):
    i = pl.program_id(0)
    tm = h_ref.shape[0]
    dff = wd_ref.shape[0]
    hf = h_ref[...]
    keep = jnp.where(i % tiles_per_seq == 0, 0.0, 1.0)
    hx_ref[0:HALO, :] = (halo_ref[...] * keep).astype(BF16)
    hx_ref[HALO:, :] = hf.astype(BF16)
    for c in range(dff // FF_CHUNK):
        conv = []
        for part in range(2):
            c0 = part * dff + c * FF_CHUNK
            u_ref[...] = _dot(hx_ref[...], wu_ref[:, c0:c0 + FF_CHUNK])
            acc = cb_ref[:, c0:c0 + FF_CHUNK]
            for tap in range(CONV_W):
                lo = HALO - (CONV_W - 1) + tap
                acc = acc + u_ref[lo:lo + tm, :] * cw_ref[tap:tap + 1, c0:c0 + FF_CHUNK]
            conv.append(acc)
        gate, val = conv
        act = gate * (1.0 / (1.0 + jnp.exp(-gate))) * val
        a_ref[:, c * FF_CHUNK:(c + 1) * FF_CHUNK] = act.astype(BF16)
    f = _dot(a_ref[...], wd_ref[...])
    o_ref[...] = _layer_norm(alpha * hf + f, g_ref[...], b_ref[...])


def _convffn(h1, wu, cw, cb, wd, g, b, alpha, S):
    T, D = h1.shape
    tm = ROW_TILE
    dff = wd.shape[0]
    per_tile = tm // HALO
    return pl.pallas_call(
        functools.partial(_convffn_kernel, alpha=alpha, tiles_per_seq=S // tm),
        grid=(T // tm,),
        in_specs=[pl.BlockSpec((tm, D), lambda i: (i, 0)),
                  pl.BlockSpec((HALO, D), lambda i: (jnp.maximum(i * per_tile - 1, 0), 0)),
                  _const_spec(wu.shape), _const_spec(cw.shape), _const_spec(cb.shape),
                  _const_spec(wd.shape), _const_spec(g.shape), _const_spec(b.shape)],
        out_specs=pl.BlockSpec((tm, D), lambda i: (i, 0)),
        out_shape=jax.ShapeDtypeStruct((T, D), F32),
        scratch_shapes=[pltpu.VMEM((tm + HALO, D), BF16), pltpu.VMEM((tm + HALO, FF_CHUNK), F32),
                        pltpu.VMEM((tm, dff), BF16)],
        compiler_params=_params(1),
        name="convffn",
    )(h1, h1, wu, cw, cb, wd, g, b)


def kernel(x, positions, w_in, b_gate, lambda_q1, lambda_k1, lambda_q2, lambda_k2, subln_g,
           rel_bias, w_proj_a, w_proj_b, w_out, ln1_g, ln1_b, w_up, conv_w, conv_b, w_down,
           ln2_g, ln2_b):
    B, S, D = x.shape
    depth = w_in.shape[0]
    alpha = (2 * depth) ** 0.25
    attn_cols = 3 * DA_HEADS * DA_HW + 3 * CB_HEADS * CB_DH
    freq = 1.0 / (ROPE_THETA ** (jnp.arange(0, DA_DH, 2, dtype=F32) / DA_DH))
    inv = jnp.tile(freq, LANES // freq.shape[0]).reshape(1, LANES)
    pos2 = positions.reshape(B * S, 1)
    rel_size = rel_bias.shape[-1]
    f_width = BAND_EXT + BAND_TQ
    row2 = lambda v: v.reshape(1, -1)

    h = x.reshape(B * S, D)
    for l in range(depth):
        lam_init = 0.8 - 0.6 * math.exp(-0.3 * l)
        w_l = w_in[l].astype(BF16)
        qt, k2, vt, qbt, kb, vbt = _inproj(h, pos2, inv, w_l[:, :attn_cols], B, S)
        oa = _diffattn(qt, k2.reshape(B, S, -1), vt, row2(lambda_q1[l]), row2(lambda_k1[l]),
                       row2(lambda_q2[l]), row2(lambda_k2[l]), subln_g[l].reshape(-1, 1), lam_init)
        lead = CB_LEFT * CHUNK
        f_ext = jnp.pad(rel_bias[l], ((0, 0), (lead, f_width - lead - rel_size)), mode="edge")
        e_tab = _bias_table(f_ext)
        ob = _bandattn(qbt, kb.reshape(B, S, -1), vbt, e_tab)
        h = _mixer(h, oa.reshape(B * S, -1), ob.reshape(B * S, -1), w_l[:, attn_cols:],
                   row2(b_gate[l]), w_proj_a[l].astype(BF16), w_proj_b[l].astype(BF16),
                   w_out[l].astype(BF16), row2(ln1_g[l]), row2(ln1_b[l]), alpha)
        h = _convffn(h, w_up[l].astype(BF16), conv_w[l], row2(conv_b[l]), w_down[l].astype(BF16),
                     row2(ln2_g[l]), row2(ln2_b[l]), alpha, S)
    return h.reshape(B, S, D)
```

```python
import functools
import math

import jax
import jax.numpy as jnp
from jax import lax
from jax.experimental import pallas as pl
from jax.experimental.pallas import tpu as pltpu

F32 = jnp.float32
BF16 = jnp.bfloat16

LANES = 128
BF16_ROWS = 16
CHUNK = 64
DA_HEADS = 8
DA_DH = 64
DA_HW = 2 * DA_DH
CB_HEADS = 8
CB_DH = 64
CB_LEFT = 8
REL_CLIP = 256
ROPE_THETA = 10000.0
LN_EPS = 1e-5
CONV_W = 3
NEG = -1e30
LOG2E = math.log2(math.e)

VMEM_LIMIT = 56 * 1024 * 1024

ROW_TILE = 512
MIX_SPLIT = 2
ATT_TILE = 512
VT_ROWS = LANES + BF16_ROWS
BAND_TQ = 256
BAND_GROUP = 4
BAND_WIN = BAND_TQ + CB_LEFT * CHUNK
BAND_EXT = BAND_WIN + CB_LEFT * CHUNK
FF_CHUNK = 256
HALO = BF16_ROWS


def _dot(a, b):
    return jnp.dot(a, b, preferred_element_type=F32)


def _const_spec(shape):
    nd = len(shape)
    return pl.BlockSpec(shape, lambda *_: (0,) * nd, pipeline_mode=pl.Buffered(1))


def _params(n_axes):
    return pltpu.CompilerParams(dimension_semantics=("arbitrary",) * n_axes,
                                vmem_limit_bytes=VMEM_LIMIT)


def _inproj_kernel(x_ref, pos_ref, inv_ref, w_ref,
                   qt_ref, k_ref, vt_ref, qbt_ref, kb_ref, vbt_ref, *, scale_a, scale_b):
    tm = x_ref.shape[0]
    xb = x_ref[...].astype(BF16)
    ang = pos_ref[...].astype(F32) * inv_ref[...]
    cos = jnp.cos(ang)
    sin = jnp.sin(ang)
    lane = lax.broadcasted_iota(jnp.int32, (tm, LANES), 1)
    first_half = (lane % DA_DH) < (DA_DH // 2)
    sin = jnp.where(first_half, -sin, sin)
    ones = jnp.ones((BF16_ROWS, tm), BF16)

    def rope(t):
        partner = jnp.where(first_half,
                            pltpu.roll(t, LANES - DA_DH // 2, axis=1),
                            pltpu.roll(t, DA_DH // 2, axis=1))
        return t * cos + partner * sin

    grp = 4 * LANES
    qk_w = DA_HEADS * DA_HW
    for g in range(qk_w // grp):
        t = _dot(xb, w_ref[:, g * grp:(g + 1) * grp])
        for j in range(4):
            r = rope(t[:, j * LANES:(j + 1) * LANES]) * scale_a
            qt_ref[0, g * 4 + j] = r.T.astype(BF16)
    for g in range(qk_w // grp):
        t = _dot(xb, w_ref[:, qk_w + g * grp: qk_w + (g + 1) * grp])
        for j in range(4):
            c0 = g * grp + j * LANES
            k_ref[:, c0:c0 + LANES] = rope(t[:, j * LANES:(j + 1) * LANES]).astype(BF16)
    for g in range(qk_w // grp):
        t = _dot(xb, w_ref[:, 2 * qk_w + g * grp: 2 * qk_w + (g + 1) * grp])
        for j in range(4):
            vt_ref[0, g * 4 + j, 0:LANES, :] = t[:, j * LANES:(j + 1) * LANES].T.astype(BF16)
            vt_ref[0, g * 4 + j, LANES:VT_ROWS, :] = ones
    off = 3 * qk_w
    cbw = CB_HEADS * CB_DH
    t = _dot(xb, w_ref[:, off:off + cbw]) * scale_b
    for j in range(cbw // LANES):
        qbt_ref[0, j] = t[:, j * LANES:(j + 1) * LANES].T.astype(BF16)
    kb_ref[...] = _dot(xb, w_ref[:, off + cbw:off + 2 * cbw]).astype(BF16)
    t = _dot(xb, w_ref[:, off + 2 * cbw:off + 3 * cbw])
    for j in range(cbw // LANES):
        vbt_ref[0, j, 0:LANES, :] = t[:, j * LANES:(j + 1) * LANES].T.astype(BF16)
        vbt_ref[0, j, LANES:VT_ROWS, :] = ones


def _inproj(x2, pos2, inv, w_attn, B, S):
    T, D = x2.shape
    tm = ROW_TILE
    spb = S // tm
    qk_w = DA_HEADS * DA_HW
    cbw = CB_HEADS * CB_DH
    pairs = cbw // LANES

    def t_spec(heads, rows):
        return pl.BlockSpec((1, heads, rows, tm), lambda i: (i // spb, 0, 0, i % spb))

    def t_shape(heads, rows):
        return jax.ShapeDtypeStruct((B, heads, rows, S), BF16)

    row = lambda w: pl.BlockSpec((tm, w), lambda i: (i, 0))
    return pl.pallas_call(
        functools.partial(_inproj_kernel, scale_a=DA_DH ** -0.5 * LOG2E,
                          scale_b=CB_DH ** -0.5 * LOG2E),
        grid=(T // tm,),
        in_specs=[row(D), row(1), _const_spec((1, LANES)), _const_spec(w_attn.shape)],
        out_specs=[t_spec(DA_HEADS, LANES), row(qk_w), t_spec(DA_HEADS, VT_ROWS),
                   t_spec(pairs, LANES), row(cbw), t_spec(pairs, VT_ROWS)],
        out_shape=[t_shape(DA_HEADS, LANES), jax.ShapeDtypeStruct((T, qk_w), BF16),
                   t_shape(DA_HEADS, VT_ROWS), t_shape(pairs, LANES),
                   jax.ShapeDtypeStruct((T, cbw), BF16), t_shape(pairs, VT_ROWS)],
        compiler_params=_params(1),
        name="inproj",
    )(x2, pos2, inv, w_attn)


def _diffattn_kernel(qt_ref, k_ref, vt_ref, lq1_ref, lk1_ref, lq2_ref, lk2_ref, g_ref,
                     o_ref, m_ref, acc_ref, sa_ref, sb_ref, ta_ref, tb_ref, *, lam_init, tile):
    S = qt_ref.shape[1]
    tq = tk = tile
    nq = S // tq
    n_tiles = nq * (nq + 1) // 2
    assert n_tiles % 2 == 0
    row = lax.broadcasted_iota(jnp.int32, (DA_HW, tq), 0)
    kc = lax.broadcasted_iota(jnp.int32, (tk, tq), 0) // CHUNK
    qc = lax.broadcasted_iota(jnp.int32, (tk, tq), 1) // CHUNK
    diag = kc <= qc

    def reset():
        m_ref[...] = jnp.full(m_ref.shape, NEG, F32)
        acc_ref[...] = jnp.zeros(acc_ref.shape, F32)

    def scores(qi, kt, s_ref, t_ref, masked):
        qt = qt_ref[:, pl.ds(pl.multiple_of(qi * tq, tq), tq)]
        kblk = k_ref[pl.ds(pl.multiple_of(kt * tk, tk), tk), :]
        zero = jnp.zeros_like(qt)
        for t in range(2):
            sub = (row < DA_DH) if t == 0 else (row >= DA_DH)
            s = _dot(kblk, jnp.where(sub, qt, zero))
            if masked:
                s = jnp.where(diag, s, NEG)
            s_ref[t] = s
            t_ref[t] = jnp.max(s, axis=0, keepdims=True)

    def consume(kt, s_ref, t_ref):
        vblk = vt_ref[:, pl.ds(pl.multiple_of(kt * tk, tk), tk)]
        for t in range(2):
            m_old = m_ref[t]
            m_new = jnp.maximum(m_old, t_ref[t])
            alpha = jnp.exp2(m_old - m_new)
            p = jnp.exp2((s_ref[t] - m_new).astype(BF16))
            acc_ref[t] = alpha * acc_ref[t] + _dot(vblk, p)
            m_ref[t] = m_new

    def finalize(qi):
        lam = (jnp.exp(jnp.sum(lq1_ref[...] * lk1_ref[...], axis=1, keepdims=True))
               - jnp.exp(jnp.sum(lq2_ref[...] * lk2_ref[...], axis=1, keepdims=True)) + lam_init)
        a0 = acc_ref[0]
        a1 = acc_ref[1]
        o = (a0[:DA_HW] / a0[DA_HW:DA_HW + 1]
             - lam * (a1[:DA_HW] / a1[DA_HW:DA_HW + 1]))
        ms = jnp.mean(o * o, axis=0, keepdims=True)
        o = o * lax.rsqrt(ms + LN_EPS) * g_ref[...] * (1.0 - lam_init)
        o_ref[pl.ds(pl.multiple_of(qi * tq, tq), tq), :] = o.T.astype(BF16)
        reset()

    def half(qi, kt, cur, nxt):
        is_diag = kt == qi
        nqi = jnp.where(is_diag, jnp.minimum(qi + 1, nq - 1), qi)
        nkt = jnp.where(is_diag, 0, kt + 1)
        next_diag = jnp.logical_and(jnp.logical_not(is_diag), nkt == nqi)

        @pl.when(jnp.logical_and(jnp.logical_not(is_diag), jnp.logical_not(next_diag)))
        def _():
            scores(nqi, nkt, *nxt, masked=False)
            consume(kt, *cur)

        @pl.when(next_diag)
        def _():
            scores(nqi, nkt, *nxt, masked=True)
            consume(kt, *cur)

        @pl.when(is_diag)
        def _():
            scores(nqi, nkt, *nxt, masked=False)
            consume(kt, *cur)
            finalize(qi)

        return nqi, nkt

    buf_a = (sa_ref, ta_ref)
    buf_b = (sb_ref, tb_ref)
    reset()
    scores(0, 0, *buf_a, masked=True)

    def pair(_, carry):
        qi, kt = carry
        qi, kt = half(qi, kt, buf_a, buf_b)
        return half(qi, kt, buf_b, buf_a)

    lax.fori_loop(0, n_tiles // 2, pair, (jnp.int32(0), jnp.int32(0)))


def _diffattn(qt, k3, vt, lq1, lk1, lq2, lk2, g_col, lam_init):
    B, H, hw, S = qt.shape
    tq = ATT_TILE
    vec = _const_spec((1, DA_DH))
    return pl.pallas_call(
        functools.partial(_diffattn_kernel, lam_init=lam_init, tile=tq),
        grid=(B, H),
        in_specs=[pl.BlockSpec((None, None, hw, S), lambda b, h: (b, h, 0, 0)),
                  pl.BlockSpec((None, S, hw), lambda b, h: (b, 0, h)),
                  pl.BlockSpec((None, None, VT_ROWS, S), lambda b, h: (b, h, 0, 0)),
                  vec, vec, vec, vec, _const_spec((hw, 1))],
        out_specs=pl.BlockSpec((None, S, hw), lambda b, h: (b, 0, h)),
        out_shape=jax.ShapeDtypeStruct((B, S, H * hw), BF16),
        scratch_shapes=[pltpu.VMEM((2, 1, tq), F32), pltpu.VMEM((2, VT_ROWS, tq), F32),
                        pltpu.VMEM((2, tq, tq), F32), pltpu.VMEM((2, tq, tq), F32),
                        pltpu.VMEM((2, 1, tq), F32), pltpu.VMEM((2, 1, tq), F32)],
        compiler_params=_params(2),
        name="diffattn",
    )(qt, k3, vt, lq1, lk1, lq2, lk2, g_col)


def _bias_kernel(f_ref, e_ref):
    ext, tq = e_ref.shape
    width = f_ref.shape[-1]
    g = jnp.broadcast_to(f_ref[...], (tq, width))
    g = pltpu.roll(g, width - REL_CLIP, axis=1, stride=1, stride_axis=0)
    bias = g[:, :ext] * LOG2E
    r = lax.broadcasted_iota(jnp.int32, (tq, ext), 0)
    u = lax.broadcasted_iota(jnp.int32, (tq, ext), 1)
    back = r // CHUNK + CB_LEFT - u // CHUNK
    e_ref[...] = jnp.where((back >= 0) & (back <= CB_LEFT), bias, NEG).T


def _bias_table(f_ext):
    H, width = f_ext.shape
    return pl.pallas_call(
        _bias_kernel,
        grid=(H,),
        in_specs=[pl.BlockSpec((None, 1, width), lambda h: (h, 0, 0))],
        out_specs=pl.BlockSpec((None, BAND_EXT, BAND_TQ), lambda h: (h, 0, 0)),
        out_shape=jax.ShapeDtypeStruct((H, BAND_EXT, BAND_TQ), F32),
        compiler_params=_params(1),
        name="bias_table",
    )(f_ext.reshape(H, 1, width))


def _bandattn_kernel(qt_ref, k_ref, vt_ref, e_ref, o_ref):
    tq = BAND_TQ
    lead = CB_LEFT * CHUNK
    row = lax.broadcasted_iota(jnp.int32, (LANES, tq), 0)
    chains = [(g, hh) for g in range(BAND_GROUP) for hh in range(2)]

    def window(g):
        want = (pl.program_id(2) * BAND_GROUP + g) * tq - lead
        start = pl.multiple_of(jnp.maximum(want, 0), tq)
        return start, pl.multiple_of(start - want, tq)

    def scores(g, hh):
        start, delta = window(g)
        kw = k_ref[pl.ds(start, BAND_WIN), :]
        qt = qt_ref[:, g * tq:(g + 1) * tq]
        sel = (row < CB_DH) if hh == 0 else (row >= CB_DH)
        return (_dot(kw, jnp.where(sel, qt, jnp.zeros_like(qt)))
                + e_ref[hh, pl.ds(delta, BAND_WIN), :])

    def softmax(s):
        return jnp.exp2((s - jnp.max(s, axis=0, keepdims=True)).astype(BF16))

    def pv(g, p):
        start, _ = window(g)
        r = _dot(vt_ref[:, pl.ds(start, BAND_WIN)], p)
        return r[:LANES] / r[LANES:LANES + 1]

    n = len(chains)
    s, p, o = {}, {}, {}
    s[0] = scores(*chains[0])
    s[1] = scores(*chains[1])
    p[0] = softmax(s.pop(0))
    for i in range(n):
        if i + 2 < n:
            s[i + 2] = scores(*chains[i + 2])
        if i + 1 < n:
            p[i + 1] = softmax(s.pop(i + 1))
        g, hh = chains[i]
        o[hh] = pv(g, p.pop(i))
        if hh == 1:
            both = jnp.where(row < CB_DH, o[0], o[1])
            o_ref[g * tq:(g + 1) * tq, :] = both.T.astype(BF16)


def _bandattn(qbt, kb3, vbt, e_tab):
    B, pairs, _, S = qbt.shape
    tq = BAND_TQ * BAND_GROUP
    return pl.pallas_call(
        _bandattn_kernel,
        grid=(B, pairs, S // tq),
        in_specs=[pl.BlockSpec((None, None, LANES, tq), lambda b, h, q: (b, h, 0, q)),
                  pl.BlockSpec((None, S, LANES), lambda b, h, q: (b, 0, h)),
                  pl.BlockSpec((None, None, VT_ROWS, S), lambda b, h, q: (b, h, 0, 0)),
                  pl.BlockSpec((2, BAND_EXT, BAND_TQ), lambda b, h, q: (h, 0, 0))],
        out_specs=pl.BlockSpec((None, tq, LANES), lambda b, h, q: (b, q, h)),
        out_shape=jax.ShapeDtypeStruct((B, S, pairs * LANES), BF16),
        compiler_params=_params(3),
        name="bandattn",
    )(qbt, kb3, vbt, e_tab)


def _layer_norm(z, g, b):
    mu = jnp.mean(z, axis=-1, keepdims=True)
    zc = z - mu
    var = jnp.mean(zc * zc, axis=-1, keepdims=True)
    return zc * lax.rsqrt(var + LN_EPS) * g + b


def _mixer_kernel(x_ref, oa_ref, ob_ref, wg_ref, bg_ref, wpa_ref, wpb_ref, wo_ref,
                  g_ref, b_ref, h_ref, *, alpha):
    d = x_ref.shape[1]
    sub = x_ref.shape[0] // MIX_SPLIT
    merged = []
    for r in range(MIX_SPLIT):
        rows = slice(r * sub, (r + 1) * sub)
        gates = _dot(x_ref[rows, :].astype(BF16), wg_ref[...]) + bg_ref[...]
        gates = 1.0 / (1.0 + jnp.exp(-gates))
        ya = _dot(oa_ref[rows, :], wpa_ref[...])
        yb = _dot(ob_ref[rows, :], wpb_ref[...])
        merged.append((gates[:, :d] * ya + gates[:, d:] * yb).astype(BF16))
    for r in range(MIX_SPLIT):
        rows = slice(r * sub, (r + 1) * sub)
        mix = _dot(merged[r], wo_ref[...])
        h_ref[rows, :] = _layer_norm(alpha * x_ref[rows, :] + mix, g_ref[...], b_ref[...])


def _mixer(x2, oa, ob, wg, bg, wpa, wpb, wo, g, b, alpha):
    T, D = x2.shape
    tm = ROW_TILE * MIX_SPLIT
    row = lambda w: pl.BlockSpec((tm, w), lambda i: (i, 0))
    return pl.pallas_call(
        functools.partial(_mixer_kernel, alpha=alpha),
        grid=(T // tm,),
        in_specs=[row(D), row(oa.shape[1]), row(ob.shape[1]), _const_spec(wg.shape),
                  _const_spec(bg.shape), _const_spec(wpa.shape), _const_spec(wpb.shape),
                  _const_spec(wo.shape), _const_spec(g.shape), _const_spec(b.shape)],
        out_specs=row(D),
        out_shape=jax.ShapeDtypeStruct((T, D), F32),
        compiler_params=_params(1),
        name="mixer",
    )(x2, oa, ob, wg, bg, wpa, wpb, wo, g, b)


def _convffn_kernel(h_ref, halo_ref, wu_ref, cw_ref, cb_ref, wd_ref, g_ref, b_ref,
                    o_ref, hx_ref, u_ref, a_ref, *, alpha, tiles_per_seq):
    i = pl.program_id(0)
    tm = h_ref.shape[0]
    dff = wd_ref.shape[0]
    hf = h_ref[...]
    keep = jnp.where(i % tiles_per_seq == 0, 0.0, 1.0)
    hx_ref[0:HALO, :] = (halo_ref[...] * keep).astype(BF16)
    hx_ref[HALO:, :] = hf.astype(BF16)
    for c in range(dff // FF_CHUNK):
        conv = []
        for part in range(2):
            c0 = part * dff + c * FF_CHUNK
            u_ref[...] = _dot(hx_ref[...], wu_ref[:, c0:c0 + FF_CHUNK])
            acc = cb_ref[:, c0:c0 + FF_CHUNK]
            for tap in range(CONV_W):
                lo = HALO - (CONV_W - 1) + tap
                acc = acc + u_ref[lo:lo + tm, :] * cw_ref[tap:tap + 1, c0:c0 + FF_CHUNK]
            conv.append(acc)
        gate, val = conv
        act = gate * (1.0 / (1.0 + jnp.exp(-gate))) * val
        a_ref[:, c * FF_CHUNK:(c + 1) * FF_CHUNK] = act.astype(BF16)
    f = _dot(a_ref[...], wd_ref[...])
    o_ref[...] = _layer_norm(alpha * hf + f, g_ref[...], b_ref[...])


def _convffn(h1, wu, cw, cb, wd, g, b, alpha, S):
    T, D = h1.shape
    tm = ROW_TILE
    dff = wd.shape[0]
    per_tile = tm // HALO
    return pl.pallas_call(
        functools.partial(_convffn_kernel, alpha=alpha, tiles_per_seq=S // tm),
        grid=(T // tm,),
        in_specs=[pl.BlockSpec((tm, D), lambda i: (i, 0)),
                  pl.BlockSpec((HALO, D), lambda i: (jnp.maximum(i * per_tile - 1, 0), 0)),
                  _const_spec(wu.shape), _const_spec(cw.shape), _const_spec(cb.shape),
                  _const_spec(wd.shape), _const_spec(g.shape), _const_spec(b.shape)],
        out_specs=pl.BlockSpec((tm, D), lambda i: (i, 0)),
        out_shape=jax.ShapeDtypeStruct((T, D), F32),
        scratch_shapes=[pltpu.VMEM((tm + HALO, D), BF16), pltpu.VMEM((tm + HALO, FF_CHUNK), F32),
                        pltpu.VMEM((tm, dff), BF16)],
        compiler_params=_params(1),
        name="convffn",
    )(h1, h1, wu, cw, cb, wd, g, b)


def kernel(x, positions, w_in, b_gate, lambda_q1, lambda_k1, lambda_q2, lambda_k2, subln_g,
           rel_bias, w_proj_a, w_proj_b, w_out, ln1_g, ln1_b, w_up, conv_w, conv_b, w_down,
           ln2_g, ln2_b):
    B, S, D = x.shape
    depth = w_in.shape[0]
    alpha = (2 * depth) ** 0.25
    attn_cols = 3 * DA_HEADS * DA_HW + 3 * CB_HEADS * CB_DH
    freq = 1.0 / (ROPE_THETA ** (jnp.arange(0, DA_DH, 2, dtype=F32) / DA_DH))
    inv = jnp.tile(freq, LANES // freq.shape[0]).reshape(1, LANES)
    pos2 = positions.reshape(B * S, 1)
    rel_size = rel_bias.shape[-1]
    f_width = BAND_EXT + BAND_TQ
    row2 = lambda v: v.reshape(1, -1)

    h = x.reshape(B * S, D)
    for l in range(depth):
        lam_init = 0.8 - 0.6 * math.exp(-0.3 * l)
        w_attn = w_in[l, :, :attn_cols].astype(BF16)
        w_gate = w_in[l, :, attn_cols:].astype(BF16)
        qt, k2, vt, qbt, kb, vbt = _inproj(h, pos2, inv, w_attn, B, S)
        oa = _diffattn(qt, k2.reshape(B, S, -1), vt, row2(lambda_q1[l]), row2(lambda_k1[l]),
                       row2(lambda_q2[l]), row2(lambda_k2[l]), subln_g[l].reshape(-1, 1), lam_init)
        lead = CB_LEFT * CHUNK
        f_ext = jnp.pad(rel_bias[l], ((0, 0), (lead, f_width - lead - rel_size)), mode="edge")
        e_tab = _bias_table(f_ext)
        ob = _bandattn(qbt, kb.reshape(B, S, -1), vbt, e_tab)
        h = _mixer(h, oa.reshape(B * S, -1), ob.reshape(B * S, -1), w_gate,
                   row2(b_gate[l]), w_proj_a[l].astype(BF16), w_proj_b[l].astype(BF16),
                   w_out[l].astype(BF16), row2(ln1_g[l]), row2(ln1_b[l]), alpha)
        h = _convffn(h, w_up[l].astype(BF16), conv_w[l], row2(conv_b[l]), w_down[l].astype(BF16),
                     row2(ln2_g[l]), row2(ln2_b[l]), alpha, S)
    return h.reshape(B, S, D)
```

```python
import functools
import math

import jax
import jax.numpy as jnp
from jax import lax
from jax.experimental import pallas as pl
from jax.experimental.pallas import tpu as pltpu

F32 = jnp.float32
BF16 = jnp.bfloat16

LANES = 128
BF16_ROWS = 16
CHUNK = 64
DA_HEADS = 8
DA_DH = 64
DA_HW = 2 * DA_DH
CB_HEADS = 8
CB_DH = 64
CB_LEFT = 8
REL_CLIP = 256
ROPE_THETA = 10000.0
LN_EPS = 1e-5
CONV_W = 3
NEG = -1e30
LOG2E = math.log2(math.e)

VMEM_LIMIT = 56 * 1024 * 1024

ROW_TILE = 512
MIX_SPLIT = 2
ATT_TILE = 512
ATT_QMULT = 1
VT_ROWS = LANES + BF16_ROWS
BAND_TQ = 256
BAND_GROUP = 4
BAND_WIN = BAND_TQ + CB_LEFT * CHUNK
BAND_EXT = BAND_WIN + CB_LEFT * CHUNK
FF_CHUNK = 256
HALO = BF16_ROWS


def _dot(a, b):
    return jnp.dot(a, b, preferred_element_type=F32)


def _rope_layout(w):
    d = w.shape[0]
    return w.reshape(d, DA_HEADS, 2, 2, DA_DH // 2).transpose(0, 1, 3, 2, 4).reshape(d, -1)


def _const_spec(shape):
    nd = len(shape)
    return pl.BlockSpec(shape, lambda *_: (0,) * nd, pipeline_mode=pl.Buffered(1))


def _params(n_axes):
    return pltpu.CompilerParams(dimension_semantics=("arbitrary",) * n_axes,
                                vmem_limit_bytes=VMEM_LIMIT)


def _inproj_kernel(x_ref, pos_ref, inv_ref, w_ref,
                   qt_ref, k_ref, vt_ref, qbt_ref, kb_ref, vbt_ref, xb_ref, trig_ref,
                   *, scale_a, scale_b):
    tm = x_ref.shape[0]
    grp = 4 * LANES
    qk_w = DA_HEADS * DA_HW
    cbw = CB_HEADS * CB_DH
    lane = lax.broadcasted_iota(jnp.int32, (tm, LANES), 1)
    first_half = lane < LANES // 2
    step = pl.program_id(0)

    @pl.when(step >= 0)
    def _():
        xb = x_ref[...].astype(BF16)
        xb_ref[...] = xb
        ones = jnp.ones((BF16_ROWS, tm), BF16)
        ang = pos_ref[...].astype(F32) * inv_ref[...]
        sin = jnp.sin(ang)
        trig_ref[0] = jnp.cos(ang)
        trig_ref[1] = jnp.where(first_half, -sin, sin)
        for g in range(qk_w // grp):
            t = _dot(xb, w_ref[:, 2 * qk_w + g * grp: 2 * qk_w + (g + 1) * grp])
            for j in range(4):
                vt_ref[0, g * 4 + j, 0:LANES, :] = t[:, j * LANES:(j + 1) * LANES].T.astype(BF16)
                vt_ref[0, g * 4 + j, LANES:VT_ROWS, :] = ones
        off = 3 * qk_w
        t = _dot(xb, w_ref[:, off:off + cbw]) * scale_b
        for j in range(cbw // LANES):
            qbt_ref[0, j] = t[:, j * LANES:(j + 1) * LANES].T.astype(BF16)
        kb_ref[...] = _dot(xb, w_ref[:, off + cbw:off + 2 * cbw]).astype(BF16)
        t = _dot(xb, w_ref[:, off + 2 * cbw:off + 3 * cbw])
        for j in range(cbw // LANES):
            vbt_ref[0, j, 0:LANES, :] = t[:, j * LANES:(j + 1) * LANES].T.astype(BF16)
            vbt_ref[0, j, LANES:VT_ROWS, :] = ones

    @pl.when(step >= 0)
    def _():
        xb = xb_ref[...]
        cos = trig_ref[0]
        sin = trig_ref[1]

        def rope(t):
            return t * cos + pltpu.roll(t, LANES // 2, axis=1) * sin

        for g in range(qk_w // grp):
            t = _dot(xb, w_ref[:, g * grp:(g + 1) * grp])
            for j in range(4):
                r = rope(t[:, j * LANES:(j + 1) * LANES]) * scale_a
                qt_ref[0, g * 4 + j] = r.T.astype(BF16)
        for g in range(qk_w // grp):
            t = _dot(xb, w_ref[:, qk_w + g * grp: qk_w + (g + 1) * grp])
            for j in range(4):
                c0 = g * grp + j * LANES
                k_ref[:, c0:c0 + LANES] = rope(t[:, j * LANES:(j + 1) * LANES]).astype(BF16)


def _inproj(x2, pos2, inv, w_attn, B, S):
    T, D = x2.shape
    tm = ROW_TILE
    spb = S // tm
    qk_w = DA_HEADS * DA_HW
    cbw = CB_HEADS * CB_DH
    pairs = cbw // LANES

    def t_spec(heads, rows):
        return pl.BlockSpec((1, heads, rows, tm), lambda i: (i // spb, 0, 0, i % spb))

    def t_shape(heads, rows):
        return jax.ShapeDtypeStruct((B, heads, rows, S), BF16)

    row = lambda w: pl.BlockSpec((tm, w), lambda i: (i, 0))
    return pl.pallas_call(
        functools.partial(_inproj_kernel, scale_a=DA_DH ** -0.5 * LOG2E,
                          scale_b=CB_DH ** -0.5 * LOG2E),
        grid=(T // tm,),
        in_specs=[row(D), row(1), _const_spec((1, LANES)), _const_spec(w_attn.shape)],
        out_specs=[t_spec(DA_HEADS, LANES), row(qk_w), t_spec(DA_HEADS, VT_ROWS),
                   t_spec(pairs, LANES), row(cbw), t_spec(pairs, VT_ROWS)],
        out_shape=[t_shape(DA_HEADS, LANES), jax.ShapeDtypeStruct((T, qk_w), BF16),
                   t_shape(DA_HEADS, VT_ROWS), t_shape(pairs, LANES),
                   jax.ShapeDtypeStruct((T, cbw), BF16), t_shape(pairs, VT_ROWS)],
        scratch_shapes=[pltpu.VMEM((tm, D), BF16), pltpu.VMEM((2, tm, LANES), F32)],
        compiler_params=_params(1),
        name="inproj",
    )(x2, pos2, inv, w_attn)


def _diffattn_kernel(qt_ref, k_ref, vt_ref, lq1_ref, lk1_ref, lq2_ref, lk2_ref, g_ref,
                     o_ref, m_ref, acc_ref, sa_ref, sb_ref, ta_ref, tb_ref,
                     *, lam_init, tk, q_mult):
    S = qt_ref.shape[1]
    tq = q_mult * tk
    nq = S // tq
    n_tiles = q_mult * nq * (nq + 1) // 2
    assert n_tiles % 2 == 0
    row = lax.broadcasted_iota(jnp.int32, (DA_HW, tq), 0)
    kc = lax.broadcasted_iota(jnp.int32, (tk, tq), 0) // CHUNK
    qc = lax.broadcasted_iota(jnp.int32, (tk, tq), 1) // CHUNK
    visible = [kc + d * (tk // CHUNK) <= qc for d in range(q_mult)]

    def reset():
        m_ref[...] = jnp.full(m_ref.shape, NEG, F32)
        acc_ref[...] = jnp.zeros(acc_ref.shape, F32)

    def scores(qi, kt, s_ref, t_ref, masked):
        qt = qt_ref[:, pl.ds(pl.multiple_of(qi * tq, tq), tq)]
        kblk = k_ref[pl.ds(pl.multiple_of(kt * tk, tk), tk), :]
        zero = jnp.zeros_like(qt)
        for t in range(2):
            sub = (row // (DA_DH // 2)) % 2 == t
            s = _dot(kblk, jnp.where(sub, qt, zero))
            if masked is not None:
                s = jnp.where(visible[masked], s, NEG)
            s_ref[t] = s
            t_ref[t] = jnp.max(s, axis=0, keepdims=True)

    def consume(kt, s_ref, t_ref):
        vblk = vt_ref[:, pl.ds(pl.multiple_of(kt * tk, tk), tk)]
        for t in range(2):
            m_old = m_ref[t]
            m_new = jnp.maximum(m_old, t_ref[t])
            alpha = jnp.exp2(m_old - m_new)
            p = jnp.exp2((s_ref[t] - m_new).astype(BF16))
            acc_ref[t] = alpha * acc_ref[t] + _dot(vblk, p)
            m_ref[t] = m_new

    def finalize(qi):
        lam = (jnp.exp(jnp.sum(lq1_ref[...] * lk1_ref[...], axis=1, keepdims=True))
               - jnp.exp(jnp.sum(lq2_ref[...] * lk2_ref[...], axis=1, keepdims=True)) + lam_init)
        a0 = acc_ref[0]
        a1 = acc_ref[1]
        o = (a0[:DA_HW] * (1.0 / a0[DA_HW:DA_HW + 1])
             - a1[:DA_HW] * (lam / a1[DA_HW:DA_HW + 1]))
        ms = jnp.mean(o * o, axis=0, keepdims=True)
        o = o * lax.rsqrt(ms + LN_EPS) * g_ref[...] * (1.0 - lam_init)
        o_ref[pl.ds(pl.multiple_of(qi * tq, tq), tq), :] = o.T.astype(BF16)
        reset()

    def half(qi, kt, cur, nxt):
        first_masked = q_mult * qi
        is_last = kt == first_masked + q_mult - 1
        nqi = jnp.where(is_last, jnp.minimum(qi + 1, nq - 1), qi)
        nkt = jnp.where(is_last, 0, kt + 1)
        next_d = jnp.where(is_last, -1, kt + 1 - first_masked)

        @pl.when(jnp.logical_and(jnp.logical_not(is_last), next_d < 0))
        def _():
            scores(nqi, nkt, *nxt, masked=None)
            consume(kt, *cur)

        for d in range(q_mult):
            @pl.when(next_d == d)
            def _():
                scores(nqi, nkt, *nxt, masked=d)
                consume(kt, *cur)

        @pl.when(is_last)
        def _():
            scores(nqi, nkt, *nxt, masked=None)
            consume(kt, *cur)
            finalize(qi)

        return nqi, nkt

    buf_a = (sa_ref, ta_ref)
    buf_b = (sb_ref, tb_ref)
    reset()
    scores(0, 0, *buf_a, masked=0)

    def pair(_, carry):
        qi, kt = carry
        qi, kt = half(qi, kt, buf_a, buf_b)
        return half(qi, kt, buf_b, buf_a)

    lax.fori_loop(0, n_tiles // 2, pair, (jnp.int32(0), jnp.int32(0)))


def _diffattn(qt, k3, vt, lq1, lk1, lq2, lk2, g_col, lam_init):
    B, H, hw, S = qt.shape
    tk = ATT_TILE
    tq = ATT_QMULT * tk
    vec = _const_spec((1, DA_DH))
    return pl.pallas_call(
        functools.partial(_diffattn_kernel, lam_init=lam_init, tk=tk, q_mult=ATT_QMULT),
        grid=(B, H),
        in_specs=[pl.BlockSpec((None, None, hw, S), lambda b, h: (b, h, 0, 0)),
                  pl.BlockSpec((None, S, hw), lambda b, h: (b, 0, h)),
                  pl.BlockSpec((None, None, VT_ROWS, S), lambda b, h: (b, h, 0, 0)),
                  vec, vec, vec, vec, _const_spec((hw, 1))],
        out_specs=pl.BlockSpec((None, S, hw), lambda b, h: (b, 0, h)),
        out_shape=jax.ShapeDtypeStruct((B, S, H * hw), BF16),
        scratch_shapes=[pltpu.VMEM((2, 1, tq), F32), pltpu.VMEM((2, VT_ROWS, tq), F32),
                        pltpu.VMEM((2, tk, tq), F32), pltpu.VMEM((2, tk, tq), F32),
                        pltpu.VMEM((2, 1, tq), F32), pltpu.VMEM((2, 1, tq), F32)],
        compiler_params=_params(2),
        name="diffattn",
    )(qt, k3, vt, lq1, lk1, lq2, lk2, g_col)


def _bias_kernel(f_ref, e_ref):
    ext, tq = e_ref.shape
    width = f_ref.shape[-1]
    g = jnp.broadcast_to(f_ref[...], (tq, width))
    g = pltpu.roll(g, width - REL_CLIP, axis=1, stride=1, stride_axis=0)
    bias = g[:, :ext] * LOG2E
    r = lax.broadcasted_iota(jnp.int32, (tq, ext), 0)
    u = lax.broadcasted_iota(jnp.int32, (tq, ext), 1)
    back = r // CHUNK + CB_LEFT - u // CHUNK
    e_ref[...] = jnp.where((back >= 0) & (back <= CB_LEFT), bias, NEG).T


def _bias_table(f_ext):
    H, width = f_ext.shape
    return pl.pallas_call(
        _bias_kernel,
        grid=(H,),
        in_specs=[pl.BlockSpec((None, 1, width), lambda h: (h, 0, 0))],
        out_specs=pl.BlockSpec((None, BAND_EXT, BAND_TQ), lambda h: (h, 0, 0)),
        out_shape=jax.ShapeDtypeStruct((H, BAND_EXT, BAND_TQ), F32),
        compiler_params=_params(1),
        name="bias_table",
    )(f_ext.reshape(H, 1, width))


def _bandattn_kernel(qt_ref, k_ref, vt_ref, e_ref, o_ref):
    tq = BAND_TQ
    lead = CB_LEFT * CHUNK
    row = lax.broadcasted_iota(jnp.int32, (LANES, tq), 0)
    chains = [(g, hh) for g in range(BAND_GROUP) for hh in range(2)]

    def window(g):
        want = (pl.program_id(2) * BAND_GROUP + g) * tq - lead
        start = pl.multiple_of(jnp.maximum(want, 0), tq)
        return start, pl.multiple_of(start - want, tq)

    def scores(g, hh):
        start, delta = window(g)
        kw = k_ref[pl.ds(start, BAND_WIN), :]
        qt = qt_ref[:, g * tq:(g + 1) * tq]
        sel = (row < CB_DH) if hh == 0 else (row >= CB_DH)
        return (_dot(kw, jnp.where(sel, qt, jnp.zeros_like(qt)))
                + e_ref[hh, pl.ds(delta, BAND_WIN), :])

    def softmax(s):
        return jnp.exp2((s - jnp.max(s, axis=0, keepdims=True)).astype(BF16))

    def pv(g, p):
        start, _ = window(g)
        r = _dot(vt_ref[:, pl.ds(start, BAND_WIN)], p)
        return r[:LANES] / r[LANES:LANES + 1]

    n = len(chains)
    s, p, o = {}, {}, {}
    s[0] = scores(*chains[0])
    s[1] = scores(*chains[1])
    p[0] = softmax(s.pop(0))
    for i in range(n):
        if i + 2 < n:
            s[i + 2] = scores(*chains[i + 2])
        if i + 1 < n:
            p[i + 1] = softmax(s.pop(i + 1))
        g, hh = chains[i]
        o[hh] = pv(g, p.pop(i))
        if hh == 1:
            both = jnp.where(row < CB_DH, o[0], o[1])
            o_ref[g * tq:(g + 1) * tq, :] = both.T.astype(BF16)


def _bandattn(qbt, kb3, vbt, e_tab):
    B, pairs, _, S = qbt.shape
    tq = BAND_TQ * BAND_GROUP
    return pl.pallas_call(
        _bandattn_kernel,
        grid=(B, pairs, S // tq),
        in_specs=[pl.BlockSpec((None, None, LANES, tq), lambda b, h, q: (b, h, 0, q)),
                  pl.BlockSpec((None, S, LANES), lambda b, h, q: (b, 0, h)),
                  pl.BlockSpec((None, None, VT_ROWS, S), lambda b, h, q: (b, h, 0, 0)),
                  pl.BlockSpec((2, BAND_EXT, BAND_TQ), lambda b, h, q: (h, 0, 0))],
        out_specs=pl.BlockSpec((None, tq, LANES), lambda b, h, q: (b, q, h)),
        out_shape=jax.ShapeDtypeStruct((B, S, pairs * LANES), BF16),
        compiler_params=_params(3),
        name="bandattn",
    )(qbt, kb3, vbt, e_tab)


def _layer_norm(z, g, b):
    mu = jnp.mean(z, axis=-1, keepdims=True)
    zc = z - mu
    var = jnp.mean(zc * zc, axis=-1, keepdims=True)
    return zc * lax.rsqrt(var + LN_EPS) * g + b


def _mixer_kernel(x_ref, oa_ref, ob_ref, wg_ref, bg_ref, wpa_ref, wpb_ref, wo_ref,
                  g_ref, b_ref, h_ref, *, alpha):
    d = x_ref.shape[1]
    sub = x_ref.shape[0] // MIX_SPLIT
    merged = []
    for r in range(MIX_SPLIT):
        rows = slice(r * sub, (r + 1) * sub)
        gates = _dot(x_ref[rows, :].astype(BF16), wg_ref[...]) + bg_ref[...]
        gates = 1.0 / (1.0 + jnp.exp(-gates))
        ya = _dot(oa_ref[rows, :], wpa_ref[...])
        yb = _dot(ob_ref[rows, :], wpb_ref[...])
        merged.append((gates[:, :d] * ya + gates[:, d:] * yb).astype(BF16))
    for r in range(MIX_SPLIT):
        rows = slice(r * sub, (r + 1) * sub)
        mix = _dot(merged[r], wo_ref[...])
        h_ref[rows, :] = _layer_norm(alpha * x_ref[rows, :] + mix, g_ref[...], b_ref[...])


def _mixer(x2, oa, ob, wg, bg, wpa, wpb, wo, g, b, alpha):
    T, D = x2.shape
    tm = ROW_TILE * MIX_SPLIT
    row = lambda w: pl.BlockSpec((tm, w), lambda i: (i, 0))
    return pl.pallas_call(
        functools.partial(_mixer_kernel, alpha=alpha),
        grid=(T // tm,),
        in_specs=[row(D), row(oa.shape[1]), row(ob.shape[1]), _const_spec(wg.shape),
                  _const_spec(bg.shape), _const_spec(wpa.shape), _const_spec(wpb.shape),
                  _const_spec(wo.shape), _const_spec(g.shape), _const_spec(b.shape)],
        out_specs=row(D),
        out_shape=jax.ShapeDtypeStruct((T, D), F32),
        compiler_params=_params(1),
        name="mixer",
    )(x2, oa, ob, wg, bg, wpa, wpb, wo, g, b)


def _convffn_kernel(h_ref, halo_ref, wu_ref, cw_ref, cb_ref, wd_ref, g_ref, b_ref,
                    o_ref, hx_ref, u_ref, a_ref, *, alpha, tiles_per_seq):
    i = pl.program_id(0)
    tm = h_ref.shape[0]
    dff = wd_ref.shape[0]
    hf = h_ref[...]
    keep = jnp.where(i % tiles_per_seq == 0, 0.0, 1.0)
    hx_ref[0:HALO, :] = (halo_ref[...] * keep).astype(BF16)
    hx_ref[HALO:, :] = hf.astype(BF16)
    n_chunks = dff // FF_CHUNK
    split = (n_chunks + 1) // 2 * FF_CHUNK

    def up(c):
        conv = []
        for part in range(2):
            c0 = part * dff + c * FF_CHUNK
            u = _dot(hx_ref[...], wu_ref[:, c0:c0 + FF_CHUNK])
            acc = cb_ref[:, c0:c0 + FF_CHUNK]
            for tap in range(CONV_W):
                shift = CONV_W - 1 - tap
                u_ref[tap, shift:shift + tm + HALO, :] = u
                acc = acc + u_ref[tap, HALO:HALO + tm, :] * cw_ref[tap:tap + 1, c0:c0 + FF_CHUNK]
            conv.append(acc)
        gate, val = conv
        act = gate * (1.0 / (1.0 + jnp.exp(-gate))) * val
        a_ref[:, c * FF_CHUNK:(c + 1) * FF_CHUNK] = act.astype(BF16)

    f = None
    for c in range(n_chunks):
        up(c)
        if (c - 1) * FF_CHUNK == split - FF_CHUNK:
            f = _dot(a_ref[:, :split], wd_ref[:split, :])
    f = f + _dot(a_ref[:, split:], wd_ref[split:, :])
    o_ref[...] = _layer_norm(alpha * hf + f, g_ref[...], b_ref[...])


def _convffn(h1, wu, cw, cb, wd, g, b, alpha, S):
    T, D = h1.shape
    tm = ROW_TILE
    dff = wd.shape[0]
    per_tile = tm // HALO
    return pl.pallas_call(
        functools.partial(_convffn_kernel, alpha=alpha, tiles_per_seq=S // tm),
        grid=(T // tm,),
        in_specs=[pl.BlockSpec((tm, D), lambda i: (i, 0)),
                  pl.BlockSpec((HALO, D), lambda i: (jnp.maximum(i * per_tile - 1, 0), 0)),
                  _const_spec(wu.shape), _const_spec(cw.shape), _const_spec(cb.shape),
                  _const_spec(wd.shape), _const_spec(g.shape), _const_spec(b.shape)],
        out_specs=pl.BlockSpec((tm, D), lambda i: (i, 0)),
        out_shape=jax.ShapeDtypeStruct((T, D), F32),
        scratch_shapes=[pltpu.VMEM((tm + HALO, D), BF16),
                        pltpu.VMEM((CONV_W, tm + 2 * HALO, FF_CHUNK), F32),
                        pltpu.VMEM((tm, dff), BF16)],
        compiler_params=_params(1),
        name="convffn",
    )(h1, h1, wu, cw, cb, wd, g, b)


def kernel(x, positions, w_in, b_gate, lambda_q1, lambda_k1, lambda_q2, lambda_k2, subln_g,
           rel_bias, w_proj_a, w_proj_b, w_out, ln1_g, ln1_b, w_up, conv_w, conv_b, w_down,
           ln2_g, ln2_b):
    B, S, D = x.shape
    depth = w_in.shape[0]
    alpha = (2 * depth) ** 0.25
    attn_cols = 3 * DA_HEADS * DA_HW + 3 * CB_HEADS * CB_DH
    freq = 1.0 / (ROPE_THETA ** (jnp.arange(0, DA_DH, 2, dtype=F32) / DA_DH))
    inv = jnp.tile(freq, LANES // freq.shape[0]).reshape(1, LANES)
    pos2 = positions.reshape(B * S, 1)
    rel_size = rel_bias.shape[-1]
    f_width = BAND_EXT + BAND_TQ
    row2 = lambda v: v.reshape(1, -1)

    h = x.reshape(B * S, D)
    for l in range(depth):
        lam_init = 0.8 - 0.6 * math.exp(-0.3 * l)
        qk_w = DA_HEADS * DA_HW
        w_attn = jnp.concatenate([_rope_layout(w_in[l, :, :qk_w]),
                                  _rope_layout(w_in[l, :, qk_w:2 * qk_w]),
                                  w_in[l, :, 2 * qk_w:attn_cols]], axis=1).astype(BF16)
        w_gate = w_in[l, :, attn_cols:].astype(BF16)
        qt, k2, vt, qbt, kb, vbt = _inproj(h, pos2, inv, w_attn, B, S)
        oa = _diffattn(qt, k2.reshape(B, S, -1), vt, row2(lambda_q1[l]), row2(lambda_k1[l]),
                       row2(lambda_q2[l]), row2(lambda_k2[l]), subln_g[l].reshape(-1, 1), lam_init)
        lead = CB_LEFT * CHUNK
        f_ext = jnp.pad(rel_bias[l], ((0, 0), (lead, f_width - lead - rel_size)), mode="edge")
        e_tab = _bias_table(f_ext)
        ob = _bandattn(qbt, kb.reshape(B, S, -1), vbt, e_tab)
        h = _mixer(h, oa.reshape(B * S, -1), ob.reshape(B * S, -1), w_gate,
                   row2(b_gate[l]), w_proj_a[l].astype(BF16), w_proj_b[l].astype(BF16),
                   w_out[l].astype(BF16), row2(ln1_g[l]), row2(ln1_b[l]), alpha)
        h = _convffn(h, w_up[l].astype(BF16), conv_w[l], row2(conv_b[l]), w_down[l].astype(BF16),
                     row2(ln2_g[l]), row2(ln2_b[l]), alpha, S)
    return h.reshape(B, S, D)
```

```python
import functools
import math

import jax
import jax.numpy as jnp
from jax import lax
from jax.experimental import pallas as pl
from jax.experimental.pallas import tpu as pltpu

F32 = jnp.float32
BF16 = jnp.bfloat16

LANES = 128
BF16_ROWS = 16
CHUNK = 64
DA_HEADS = 8
DA_DH = 64
DA_HW = 2 * DA_DH
CB_HEADS = 8
CB_DH = 64
CB_LEFT = 8
REL_CLIP = 256
ROPE_THETA = 10000.0
LN_EPS = 1e-5
CONV_W = 3
NEG = -1e30
LOG2E = math.log2(math.e)

VMEM_LIMIT = 56 * 1024 * 1024

ROW_TILE = 512
INPROJ_TILE = 1024
MIX_SPLIT = 2
ATT_TILE = 512
ATT_QMULT = 1
VT_ROWS = LANES + BF16_ROWS
BAND_TQ = 256
BAND_GROUP = 16
BAND_WIN = BAND_TQ + CB_LEFT * CHUNK
BAND_EXT = BAND_WIN + CB_LEFT * CHUNK
FF_CHUNK = 256
HALO = BF16_ROWS


def _dot(a, b):
    return jnp.dot(a, b, preferred_element_type=F32)


def _rope_layout(w):
    d = w.shape[0]
    return w.reshape(d, DA_HEADS, 2, 2, DA_DH // 2).transpose(0, 1, 3, 2, 4).reshape(d, -1)


def _const_spec(shape):
    nd = len(shape)
    return pl.BlockSpec(shape, lambda *_: (0,) * nd, pipeline_mode=pl.Buffered(1))


def _params(n_axes):
    return pltpu.CompilerParams(dimension_semantics=("arbitrary",) * n_axes,
                                vmem_limit_bytes=VMEM_LIMIT)


def _inproj_kernel(x_ref, pos_ref, inv_ref, w_ref,
                   qt_ref, k_ref, vt_ref, qbt_ref, kb_ref, vbt_ref, xb_ref, trig_ref,
                   *, scale_a, scale_b):
    tm = x_ref.shape[0]
    grp = 4 * LANES
    qk_w = DA_HEADS * DA_HW
    cbw = CB_HEADS * CB_DH
    lane = lax.broadcasted_iota(jnp.int32, (tm, LANES), 1)
    first_half = lane < LANES // 2
    step = pl.program_id(0)

    @pl.when(step >= 0)
    def _():
        xb = x_ref[...].astype(BF16)
        xb_ref[...] = xb
        ones = jnp.ones((BF16_ROWS, tm), BF16)
        ang = pos_ref[...].astype(F32) * inv_ref[...]
        sin = jnp.sin(ang)
        trig_ref[0] = jnp.cos(ang)
        trig_ref[1] = jnp.where(first_half, -sin, sin)
        for g in range(qk_w // grp):
            t = _dot(xb, w_ref[:, 2 * qk_w + g * grp: 2 * qk_w + (g + 1) * grp])
            for j in range(4):
                vt_ref[0, g * 4 + j, 0:LANES, :] = t[:, j * LANES:(j + 1) * LANES].T.astype(BF16)
                vt_ref[0, g * 4 + j, LANES:VT_ROWS, :] = ones
        off = 3 * qk_w
        t = _dot(xb, w_ref[:, off:off + cbw]) * scale_b
        for j in range(cbw // LANES):
            qbt_ref[0, j] = t[:, j * LANES:(j + 1) * LANES].T.astype(BF16)
        kb_ref[...] = _dot(xb, w_ref[:, off + cbw:off + 2 * cbw]).astype(BF16)
        t = _dot(xb, w_ref[:, off + 2 * cbw:off + 3 * cbw])
        for j in range(cbw // LANES):
            vbt_ref[0, j, 0:LANES, :] = t[:, j * LANES:(j + 1) * LANES].T.astype(BF16)
            vbt_ref[0, j, LANES:VT_ROWS, :] = ones

    @pl.when(step >= 0)
    def _():
        xb = xb_ref[...]
        cos = trig_ref[0]
        sin = trig_ref[1]

        def rope(t):
            return t * cos + pltpu.roll(t, LANES // 2, axis=1) * sin

        for g in range(qk_w // grp):
            t = _dot(xb, w_ref[:, g * grp:(g + 1) * grp])
            for j in range(4):
                r = rope(t[:, j * LANES:(j + 1) * LANES]) * scale_a
                qt_ref[0, g * 4 + j] = r.T.astype(BF16)
        for g in range(qk_w // grp):
            t = _dot(xb, w_ref[:, qk_w + g * grp: qk_w + (g + 1) * grp])
            for j in range(4):
                c0 = g * grp + j * LANES
                k_ref[:, c0:c0 + LANES] = rope(t[:, j * LANES:(j + 1) * LANES]).astype(BF16)


def _inproj(x2, pos2, inv, w_attn, B, S):
    T, D = x2.shape
    tm = INPROJ_TILE
    spb = S // tm
    qk_w = DA_HEADS * DA_HW
    cbw = CB_HEADS * CB_DH
    pairs = cbw // LANES

    def t_spec(heads, rows):
        return pl.BlockSpec((1, heads, rows, tm), lambda i: (i // spb, 0, 0, i % spb))

    def t_shape(heads, rows):
        return jax.ShapeDtypeStruct((B, heads, rows, S), BF16)

    row = lambda w: pl.BlockSpec((tm, w), lambda i: (i, 0))
    return pl.pallas_call(
        functools.partial(_inproj_kernel, scale_a=DA_DH ** -0.5 * LOG2E,
                          scale_b=CB_DH ** -0.5 * LOG2E),
        grid=(T // tm,),
        in_specs=[row(D), row(1), _const_spec((1, LANES)), _const_spec(w_attn.shape)],
        out_specs=[t_spec(DA_HEADS, LANES), row(qk_w), t_spec(DA_HEADS, VT_ROWS),
                   t_spec(pairs, LANES), row(cbw), t_spec(pairs, VT_ROWS)],
        out_shape=[t_shape(DA_HEADS, LANES), jax.ShapeDtypeStruct((T, qk_w), BF16),
                   t_shape(DA_HEADS, VT_ROWS), t_shape(pairs, LANES),
                   jax.ShapeDtypeStruct((T, cbw), BF16), t_shape(pairs, VT_ROWS)],
        scratch_shapes=[pltpu.VMEM((tm, D), BF16), pltpu.VMEM((2, tm, LANES), F32)],
        compiler_params=_params(1),
        name="inproj",
    )(x2, pos2, inv, w_attn)


def _diffattn_kernel(qt_ref, k_ref, vt_ref, lq1_ref, lk1_ref, lq2_ref, lk2_ref, g_ref,
                     o_ref, m_ref, acc_ref, sa_ref, sb_ref, ta_ref, tb_ref,
                     *, lam_init, tk, q_mult):
    S = qt_ref.shape[1]
    tq = q_mult * tk
    nq = S // tq
    n_tiles = q_mult * nq * (nq + 1) // 2
    assert n_tiles % 2 == 0
    row = lax.broadcasted_iota(jnp.int32, (DA_HW, tq), 0)
    kc = lax.broadcasted_iota(jnp.int32, (tk, tq), 0) // CHUNK
    qc = lax.broadcasted_iota(jnp.int32, (tk, tq), 1) // CHUNK
    visible = [kc + d * (tk // CHUNK) <= qc for d in range(q_mult)]

    def reset():
        m_ref[...] = jnp.full(m_ref.shape, NEG, F32)
        acc_ref[...] = jnp.zeros(acc_ref.shape, F32)

    def scores(qi, kt, s_ref, t_ref, masked):
        qt = qt_ref[:, pl.ds(pl.multiple_of(qi * tq, tq), tq)]
        kblk = k_ref[pl.ds(pl.multiple_of(kt * tk, tk), tk), :]
        zero = jnp.zeros_like(qt)
        for t in range(2):
            sub = (row // (DA_DH // 2)) % 2 == t
            s = _dot(kblk, jnp.where(sub, qt, zero))
            if masked is not None:
                s = jnp.where(visible[masked], s, NEG)
            s_ref[t] = s
            t_ref[t] = jnp.max(s, axis=0, keepdims=True)

    def consume(kt, s_ref, t_ref):
        vblk = vt_ref[:, pl.ds(pl.multiple_of(kt * tk, tk), tk)]
        for t in range(2):
            m_old = m_ref[t]
            m_new = jnp.maximum(m_old, t_ref[t])
            alpha = jnp.exp2(m_old - m_new)
            p = jnp.exp2((s_ref[t] - m_new).astype(BF16))
            acc_ref[t] = alpha * acc_ref[t] + _dot(vblk, p)
            m_ref[t] = m_new

    def finalize(qi):
        lam = (jnp.exp(jnp.sum(lq1_ref[...] * lk1_ref[...], axis=1, keepdims=True))
               - jnp.exp(jnp.sum(lq2_ref[...] * lk2_ref[...], axis=1, keepdims=True)) + lam_init)
        a0 = acc_ref[0]
        a1 = acc_ref[1]
        o = (a0[:DA_HW] * (1.0 / a0[DA_HW:DA_HW + 1])
             - a1[:DA_HW] * (lam / a1[DA_HW:DA_HW + 1]))
        ms = jnp.mean(o * o, axis=0, keepdims=True)
        o = o * lax.rsqrt(ms + LN_EPS) * g_ref[...] * (1.0 - lam_init)
        o_ref[pl.ds(pl.multiple_of(qi * tq, tq), tq), :] = o.T.astype(BF16)
        reset()

    def half(qi, kt, cur, nxt):
        first_masked = q_mult * qi
        is_last = kt == first_masked + q_mult - 1
        nqi = jnp.where(is_last, jnp.minimum(qi + 1, nq - 1), qi)
        nkt = jnp.where(is_last, 0, kt + 1)
        next_d = jnp.where(is_last, -1, kt + 1 - first_masked)

        @pl.when(jnp.logical_and(jnp.logical_not(is_last), next_d < 0))
        def _():
            scores(nqi, nkt, *nxt, masked=None)
            consume(kt, *cur)

        for d in range(q_mult):
            @pl.when(next_d == d)
            def _():
                scores(nqi, nkt, *nxt, masked=d)
                consume(kt, *cur)

        @pl.when(is_last)
        def _():
            scores(nqi, nkt, *nxt, masked=None)
            consume(kt, *cur)
            finalize(qi)

        return nqi, nkt

    buf_a = (sa_ref, ta_ref)
    buf_b = (sb_ref, tb_ref)
    reset()
    scores(0, 0, *buf_a, masked=0)

    def pair(_, carry):
        qi, kt = carry
        qi, kt = half(qi, kt, buf_a, buf_b)
        return half(qi, kt, buf_b, buf_a)

    lax.fori_loop(0, n_tiles // 2, pair, (jnp.int32(0), jnp.int32(0)))


def _diffattn(qt, k3, vt, lq1, lk1, lq2, lk2, g_col, lam_init):
    B, H, hw, S = qt.shape
    tk = ATT_TILE
    tq = ATT_QMULT * tk
    vec = _const_spec((1, DA_DH))
    return pl.pallas_call(
        functools.partial(_diffattn_kernel, lam_init=lam_init, tk=tk, q_mult=ATT_QMULT),
        grid=(B, H),
        in_specs=[pl.BlockSpec((None, None, hw, S), lambda b, h: (b, h, 0, 0)),
                  pl.BlockSpec((None, S, hw), lambda b, h: (b, 0, h)),
                  pl.BlockSpec((None, None, VT_ROWS, S), lambda b, h: (b, h, 0, 0)),
                  vec, vec, vec, vec, _const_spec((hw, 1))],
        out_specs=pl.BlockSpec((None, S, hw), lambda b, h: (b, 0, h)),
        out_shape=jax.ShapeDtypeStruct((B, S, H * hw), BF16),
        scratch_shapes=[pltpu.VMEM((2, 1, tq), F32), pltpu.VMEM((2, VT_ROWS, tq), F32),
                        pltpu.VMEM((2, tk, tq), F32), pltpu.VMEM((2, tk, tq), F32),
                        pltpu.VMEM((2, 1, tq), F32), pltpu.VMEM((2, 1, tq), F32)],
        compiler_params=_params(2),
        name="diffattn",
    )(qt, k3, vt, lq1, lk1, lq2, lk2, g_col)


def _bias_kernel(f_ref, e_ref):
    ext, tq = e_ref.shape
    width = f_ref.shape[-1]
    g = jnp.broadcast_to(f_ref[...], (tq, width))
    g = pltpu.roll(g, width - REL_CLIP, axis=1, stride=1, stride_axis=0)
    bias = g[:, :ext] * LOG2E
    r = lax.broadcasted_iota(jnp.int32, (tq, ext), 0)
    u = lax.broadcasted_iota(jnp.int32, (tq, ext), 1)
    back = r // CHUNK + CB_LEFT - u // CHUNK
    e_ref[...] = jnp.where((back >= 0) & (back <= CB_LEFT), bias, NEG).T


def _bias_table(f_ext):
    H, width = f_ext.shape
    return pl.pallas_call(
        _bias_kernel,
        grid=(H,),
        in_specs=[pl.BlockSpec((None, 1, width), lambda h: (h, 0, 0))],
        out_specs=pl.BlockSpec((None, BAND_EXT, BAND_TQ), lambda h: (h, 0, 0)),
        out_shape=jax.ShapeDtypeStruct((H, BAND_EXT, BAND_TQ), F32),
        compiler_params=_params(1),
        name="bias_table",
    )(f_ext.reshape(H, 1, width))


def _bandattn_kernel(qt_ref, k_ref, vt_ref, e_ref, o_ref):
    tq = BAND_TQ
    lead = CB_LEFT * CHUNK
    row = lax.broadcasted_iota(jnp.int32, (LANES, tq), 0)
    chains = [(g, hh) for g in range(BAND_GROUP) for hh in range(2)]

    def window(g):
        want = (pl.program_id(2) * BAND_GROUP + g) * tq - lead
        start = pl.multiple_of(jnp.maximum(want, 0), tq)
        return start, pl.multiple_of(start - want, tq)

    def scores(g, hh):
        start, delta = window(g)
        kw = k_ref[pl.ds(start, BAND_WIN), :]
        qt = qt_ref[:, g * tq:(g + 1) * tq]
        sel = (row < CB_DH) if hh == 0 else (row >= CB_DH)
        return (_dot(kw, jnp.where(sel, qt, jnp.zeros_like(qt)))
                + e_ref[hh, pl.ds(delta, BAND_WIN), :])

    def softmax(s):
        return jnp.exp2((s - jnp.max(s, axis=0, keepdims=True)).astype(BF16))

    def pv(g, p):
        start, _ = window(g)
        r = _dot(vt_ref[:, pl.ds(start, BAND_WIN)], p)
        return r[:LANES] / r[LANES:LANES + 1]

    n = len(chains)
    s, p, o = {}, {}, {}
    s[0] = scores(*chains[0])
    s[1] = scores(*chains[1])
    p[0] = softmax(s.pop(0))
    for i in range(n):
        if i + 2 < n:
            s[i + 2] = scores(*chains[i + 2])
        if i + 1 < n:
            p[i + 1] = softmax(s.pop(i + 1))
        g, hh = chains[i]
        o[hh] = pv(g, p.pop(i))
        if hh == 1:
            both = jnp.where(row < CB_DH, o[0], o[1])
            o_ref[g * tq:(g + 1) * tq, :] = both.T.astype(BF16)


def _bandattn(qbt, kb3, vbt, e_tab):
    B, pairs, _, S = qbt.shape
    tq = BAND_TQ * BAND_GROUP
    return pl.pallas_call(
        _bandattn_kernel,
        grid=(B, pairs, S // tq),
        in_specs=[pl.BlockSpec((None, None, LANES, tq), lambda b, h, q: (b, h, 0, q)),
                  pl.BlockSpec((None, S, LANES), lambda b, h, q: (b, 0, h)),
                  pl.BlockSpec((None, None, VT_ROWS, S), lambda b, h, q: (b, h, 0, 0)),
                  pl.BlockSpec((2, BAND_EXT, BAND_TQ), lambda b, h, q: (h, 0, 0))],
        out_specs=pl.BlockSpec((None, tq, LANES), lambda b, h, q: (b, q, h)),
        out_shape=jax.ShapeDtypeStruct((B, S, pairs * LANES), BF16),
        compiler_params=_params(3),
        name="bandattn",
    )(qbt, kb3, vbt, e_tab)


def _layer_norm(z, g, b):
    mu = jnp.mean(z, axis=-1, keepdims=True)
    zc = z - mu
    var = jnp.mean(zc * zc, axis=-1, keepdims=True)
    return zc * lax.rsqrt(var + LN_EPS) * g + b


def _mixer_kernel(x_ref, oa_ref, ob_ref, wg_ref, bg_ref, wpa_ref, wpb_ref, wo_ref,
                  g_ref, b_ref, h_ref, *, alpha):
    d = x_ref.shape[1]
    sub = x_ref.shape[0] // MIX_SPLIT
    merged = []
    for r in range(MIX_SPLIT):
        rows = slice(r * sub, (r + 1) * sub)
        gates = _dot(x_ref[rows, :].astype(BF16), wg_ref[...]) + bg_ref[...]
        gates = 1.0 / (1.0 + jnp.exp(-gates))
        ya = _dot(oa_ref[rows, :], wpa_ref[...])
        yb = _dot(ob_ref[rows, :], wpb_ref[...])
        merged.append((gates[:, :d] * ya + gates[:, d:] * yb).astype(BF16))
    for r in range(MIX_SPLIT):
        rows = slice(r * sub, (r + 1) * sub)
        mix = _dot(merged[r], wo_ref[...])
        h_ref[rows, :] = _layer_norm(alpha * x_ref[rows, :] + mix, g_ref[...], b_ref[...])


def _mixer(x2, oa, ob, wg, bg, wpa, wpb, wo, g, b, alpha):
    T, D = x2.shape
    tm = ROW_TILE * MIX_SPLIT
    row = lambda w: pl.BlockSpec((tm, w), lambda i: (i, 0))
    return pl.pallas_call(
        functools.partial(_mixer_kernel, alpha=alpha),
        grid=(T // tm,),
        in_specs=[row(D), row(oa.shape[1]), row(ob.shape[1]), _const_spec(wg.shape),
                  _const_spec(bg.shape), _const_spec(wpa.shape), _const_spec(wpb.shape),
                  _const_spec(wo.shape), _const_spec(g.shape), _const_spec(b.shape)],
        out_specs=row(D),
        out_shape=jax.ShapeDtypeStruct((T, D), F32),
        compiler_params=_params(1),
        name="mixer",
    )(x2, oa, ob, wg, bg, wpa, wpb, wo, g, b)


def _convffn_kernel(h_ref, halo_ref, wu_ref, cw_ref, cb_ref, wd_ref, g_ref, b_ref,
                    o_ref, hx_ref, u_ref, a_ref, *, alpha, tiles_per_seq):
    i = pl.program_id(0)
    tm = h_ref.shape[0]
    dff = wd_ref.shape[0]
    hf = h_ref[...]
    keep = jnp.where(i % tiles_per_seq == 0, 0.0, 1.0)
    hx_ref[0:HALO, :] = (halo_ref[...] * keep).astype(BF16)
    hx_ref[HALO:, :] = hf.astype(BF16)
    n_chunks = dff // FF_CHUNK
    split = (n_chunks + 1) // 2 * FF_CHUNK

    def up(c):
        conv = []
        for part in range(2):
            c0 = part * dff + c * FF_CHUNK
            u = _dot(hx_ref[...], wu_ref[:, c0:c0 + FF_CHUNK])
            acc = cb_ref[:, c0:c0 + FF_CHUNK]
            for tap in range(CONV_W):
                shift = CONV_W - 1 - tap
                u_ref[tap, shift:shift + tm + HALO, :] = u
                acc = acc + u_ref[tap, HALO:HALO + tm, :] * cw_ref[tap:tap + 1, c0:c0 + FF_CHUNK]
            conv.append(acc)
        gate, val = conv
        act = gate * (1.0 / (1.0 + jnp.exp(-gate))) * val
        a_ref[:, c * FF_CHUNK:(c + 1) * FF_CHUNK] = act.astype(BF16)

    f = None
    for c in range(n_chunks):
        up(c)
        if (c - 1) * FF_CHUNK == split - FF_CHUNK:
            f = _dot(a_ref[:, :split], wd_ref[:split, :])
    f = f + _dot(a_ref[:, split:], wd_ref[split:, :])
    o_ref[...] = _layer_norm(alpha * hf + f, g_ref[...], b_ref[...])


def _convffn(h1, wu, cw, cb, wd, g, b, alpha, S):
    T, D = h1.shape
    tm = ROW_TILE
    dff = wd.shape[0]
    per_tile = tm // HALO
    return pl.pallas_call(
        functools.partial(_convffn_kernel, alpha=alpha, tiles_per_seq=S // tm),
        grid=(T // tm,),
        in_specs=[pl.BlockSpec((tm, D), lambda i: (i, 0)),
                  pl.BlockSpec((HALO, D), lambda i: (jnp.maximum(i * per_tile - 1, 0), 0)),
                  _const_spec(wu.shape), _const_spec(cw.shape), _const_spec(cb.shape),
                  _const_spec(wd.shape), _const_spec(g.shape), _const_spec(b.shape)],
        out_specs=pl.BlockSpec((tm, D), lambda i: (i, 0)),
        out_shape=jax.ShapeDtypeStruct((T, D), F32),
        scratch_shapes=[pltpu.VMEM((tm + HALO, D), BF16),
                        pltpu.VMEM((CONV_W, tm + 2 * HALO, FF_CHUNK), F32),
                        pltpu.VMEM((tm, dff), BF16)],
        compiler_params=_params(1),
        name="convffn",
    )(h1, h1, wu, cw, cb, wd, g, b)


def kernel(x, positions, w_in, b_gate, lambda_q1, lambda_k1, lambda_q2, lambda_k2, subln_g,
           rel_bias, w_proj_a, w_proj_b, w_out, ln1_g, ln1_b, w_up, conv_w, conv_b, w_down,
           ln2_g, ln2_b):
    B, S, D = x.shape
    depth = w_in.shape[0]
    alpha = (2 * depth) ** 0.25
    attn_cols = 3 * DA_HEADS * DA_HW + 3 * CB_HEADS * CB_DH
    freq = 1.0 / (ROPE_THETA ** (jnp.arange(0, DA_DH, 2, dtype=F32) / DA_DH))
    inv = jnp.tile(freq, LANES // freq.shape[0]).reshape(1, LANES)
    pos2 = positions.reshape(B * S, 1)
    rel_size = rel_bias.shape[-1]
    f_width = BAND_EXT + BAND_TQ
    row2 = lambda v: v.reshape(1, -1)

    h = x.reshape(B * S, D)
    for l in range(depth):
        lam_init = 0.8 - 0.6 * math.exp(-0.3 * l)
        qk_w = DA_HEADS * DA_HW
        w_attn = jnp.concatenate([_rope_layout(w_in[l, :, :qk_w]),
                                  _rope_layout(w_in[l, :, qk_w:2 * qk_w]),
                                  w_in[l, :, 2 * qk_w:attn_cols]], axis=1).astype(BF16)
        w_gate = w_in[l, :, attn_cols:].astype(BF16)
        qt, k2, vt, qbt, kb, vbt = _inproj(h, pos2, inv, w_attn, B, S)
        oa = _diffattn(qt, k2.reshape(B, S, -1), vt, row2(lambda_q1[l]), row2(lambda_k1[l]),
                       row2(lambda_q2[l]), row2(lambda_k2[l]), subln_g[l].reshape(-1, 1), lam_init)
        lead = CB_LEFT * CHUNK
        f_ext = jnp.pad(rel_bias[l], ((0, 0), (lead, f_width - lead - rel_size)), mode="edge")
        e_tab = _bias_table(f_ext)
        ob = _bandattn(qbt, kb.reshape(B, S, -1), vbt, e_tab)
        h = _mixer(h, oa.reshape(B * S, -1), ob.reshape(B * S, -1), w_gate,
                   row2(b_gate[l]), w_proj_a[l].astype(BF16), w_proj_b[l].astype(BF16),
                   w_out[l].astype(BF16), row2(ln1_g[l]), row2(ln1_b[l]), alpha)
        h = _convffn(h, w_up[l].astype(BF16), conv_w[l], row2(conv_b[l]), w_down[l].astype(BF16),
                     row2(ln2_g[l]), row2(ln2_b[l]), alpha, S)
    return h.reshape(B, S, D)
```

```python
import functools
import math

import jax
import jax.numpy as jnp
from jax import lax
from jax.experimental import pallas as pl
from jax.experimental.pallas import tpu as pltpu

F32 = jnp.float32
BF16 = jnp.bfloat16

LANES = 128
BF16_ROWS = 16
CHUNK = 64
DA_HEADS = 8
DA_DH = 64
DA_HW = 2 * DA_DH
CB_HEADS = 8
CB_DH = 64
CB_LEFT = 8
REL_CLIP = 256
ROPE_THETA = 10000.0
LN_EPS = 1e-5
CONV_W = 3
NEG = -1e30
LOG2E = math.log2(math.e)

VMEM_LIMIT = 56 * 1024 * 1024

ROW_TILE = 512
INPROJ_TILE = 1024
MIX_SPLIT = 2
ATT_TILE = 512
ATT_QMULT = 1
VT_ROWS = LANES + BF16_ROWS
BAND_TQ = 256
BAND_GROUP = 32
BAND_WIN = BAND_TQ + CB_LEFT * CHUNK
BAND_EXT = BAND_WIN + CB_LEFT * CHUNK
FF_CHUNK = 256
HALO = BF16_ROWS


def _dot(a, b):
    return jnp.dot(a, b, preferred_element_type=F32)


def _rope_layout(w):
    d = w.shape[0]
    return w.reshape(d, DA_HEADS, 2, 2, DA_DH // 2).transpose(0, 1, 3, 2, 4).reshape(d, -1)


def _const_spec(shape):
    nd = len(shape)
    return pl.BlockSpec(shape, lambda *_: (0,) * nd, pipeline_mode=pl.Buffered(1))


def _params(n_axes):
    return pltpu.CompilerParams(dimension_semantics=("arbitrary",) * n_axes,
                                vmem_limit_bytes=VMEM_LIMIT)


def _inproj_kernel(x_ref, pos_ref, inv_ref, w_ref,
                   qt_ref, k_ref, vt_ref, qbt_ref, kb_ref, vbt_ref, xb_ref, trig_ref,
                   *, scale_a, scale_b):
    tm = x_ref.shape[0]
    grp = 4 * LANES
    qk_w = DA_HEADS * DA_HW
    cbw = CB_HEADS * CB_DH
    lane = lax.broadcasted_iota(jnp.int32, (tm, LANES), 1)
    first_half = lane < LANES // 2
    step = pl.program_id(0)

    @pl.when(step >= 0)
    def _():
        xb = x_ref[...].astype(BF16)
        xb_ref[...] = xb
        ones = jnp.ones((BF16_ROWS, tm), BF16)
        ang = pos_ref[...].astype(F32) * inv_ref[...]
        sin = jnp.sin(ang)
        trig_ref[0] = jnp.cos(ang)
        trig_ref[1] = jnp.where(first_half, -sin, sin)
        for g in range(qk_w // grp):
            t = _dot(xb, w_ref[:, 2 * qk_w + g * grp: 2 * qk_w + (g + 1) * grp])
            for j in range(4):
                vt_ref[0, g * 4 + j, 0:LANES, :] = t[:, j * LANES:(j + 1) * LANES].T.astype(BF16)
                vt_ref[0, g * 4 + j, LANES:VT_ROWS, :] = ones
        off = 3 * qk_w
        t = _dot(xb, w_ref[:, off:off + cbw]) * scale_b
        for j in range(cbw // LANES):
            qbt_ref[0, j] = t[:, j * LANES:(j + 1) * LANES].T.astype(BF16)
        kb_ref[...] = _dot(xb, w_ref[:, off + cbw:off + 2 * cbw]).astype(BF16)
        t = _dot(xb, w_ref[:, off + 2 * cbw:off + 3 * cbw])
        for j in range(cbw // LANES):
            vbt_ref[0, j, 0:LANES, :] = t[:, j * LANES:(j + 1) * LANES].T.astype(BF16)
            vbt_ref[0, j, LANES:VT_ROWS, :] = ones

    @pl.when(step >= 0)
    def _():
        xb = xb_ref[...]
        cos = trig_ref[0]
        sin = trig_ref[1]

        def rope(t):
            return t * cos + pltpu.roll(t, LANES // 2, axis=1) * sin

        for g in range(qk_w // grp):
            t = _dot(xb, w_ref[:, g * grp:(g + 1) * grp])
            for j in range(4):
                r = rope(t[:, j * LANES:(j + 1) * LANES]) * scale_a
                qt_ref[0, g * 4 + j] = r.T.astype(BF16)
        for g in range(qk_w // grp):
            t = _dot(xb, w_ref[:, qk_w + g * grp: qk_w + (g + 1) * grp])
            for j in range(4):
                c0 = g * grp + j * LANES
                k_ref[:, c0:c0 + LANES] = rope(t[:, j * LANES:(j + 1) * LANES]).astype(BF16)


def _inproj(x2, pos2, inv, w_attn, B, S):
    T, D = x2.shape
    tm = INPROJ_TILE
    spb = S // tm
    qk_w = DA_HEADS * DA_HW
    cbw = CB_HEADS * CB_DH
    pairs = cbw // LANES

    def t_spec(heads, rows):
        return pl.BlockSpec((1, heads, rows, tm), lambda i: (i // spb, 0, 0, i % spb))

    def t_shape(heads, rows):
        return jax.ShapeDtypeStruct((B, heads, rows, S), BF16)

    row = lambda w: pl.BlockSpec((tm, w), lambda i: (i, 0))
    return pl.pallas_call(
        functools.partial(_inproj_kernel, scale_a=DA_DH ** -0.5 * LOG2E,
                          scale_b=CB_DH ** -0.5 * LOG2E),
        grid=(T // tm,),
        in_specs=[row(D), row(1), _const_spec((1, LANES)), _const_spec(w_attn.shape)],
        out_specs=[t_spec(DA_HEADS, LANES), row(qk_w), t_spec(DA_HEADS, VT_ROWS),
                   t_spec(pairs, LANES), row(cbw), t_spec(pairs, VT_ROWS)],
        out_shape=[t_shape(DA_HEADS, LANES), jax.ShapeDtypeStruct((T, qk_w), BF16),
                   t_shape(DA_HEADS, VT_ROWS), t_shape(pairs, LANES),
                   jax.ShapeDtypeStruct((T, cbw), BF16), t_shape(pairs, VT_ROWS)],
        scratch_shapes=[pltpu.VMEM((tm, D), BF16), pltpu.VMEM((2, tm, LANES), F32)],
        compiler_params=_params(1),
        name="inproj",
    )(x2, pos2, inv, w_attn)


def _diffattn_kernel(qt_ref, k_ref, vt_ref, lq1_ref, lk1_ref, lq2_ref, lk2_ref, g_ref,
                     o_ref, m_ref, acc_ref, sa_ref, sb_ref, ta_ref, tb_ref,
                     *, lam_init, tk, q_mult):
    S = qt_ref.shape[1]
    tq = q_mult * tk
    nq = S // tq
    n_tiles = q_mult * nq * (nq + 1) // 2
    assert n_tiles % 2 == 0
    row = lax.broadcasted_iota(jnp.int32, (DA_HW, tq), 0)
    kc = lax.broadcasted_iota(jnp.int32, (tk, tq), 0) // CHUNK
    qc = lax.broadcasted_iota(jnp.int32, (tk, tq), 1) // CHUNK
    visible = [kc + d * (tk // CHUNK) <= qc for d in range(q_mult)]

    def reset():
        m_ref[...] = jnp.full(m_ref.shape, NEG, F32)
        acc_ref[...] = jnp.zeros(acc_ref.shape, F32)

    def scores(qi, kt, s_ref, t_ref, masked):
        qt = qt_ref[:, pl.ds(pl.multiple_of(qi * tq, tq), tq)]
        kblk = k_ref[pl.ds(pl.multiple_of(kt * tk, tk), tk), :]
        zero = jnp.zeros_like(qt)
        for t in range(2):
            sub = (row // (DA_DH // 2)) % 2 == t
            s = _dot(kblk, jnp.where(sub, qt, zero))
            if masked is not None:
                s = jnp.where(visible[masked], s, NEG)
            sb = s.astype(BF16)
            s_ref[t] = sb
            t_ref[t] = jnp.max(sb, axis=0, keepdims=True).astype(F32)

    def consume(kt, s_ref, t_ref):
        vblk = vt_ref[:, pl.ds(pl.multiple_of(kt * tk, tk), tk)]
        for t in range(2):
            m_old = m_ref[t]
            m_new = jnp.maximum(m_old, t_ref[t])
            alpha = jnp.exp2(m_old - m_new)
            p = jnp.exp2(s_ref[t] - m_new.astype(BF16))
            acc_ref[t] = alpha * acc_ref[t] + _dot(vblk, p)
            m_ref[t] = m_new

    def finalize(qi):
        lam = (jnp.exp(jnp.sum(lq1_ref[...] * lk1_ref[...], axis=1, keepdims=True))
               - jnp.exp(jnp.sum(lq2_ref[...] * lk2_ref[...], axis=1, keepdims=True)) + lam_init)
        a0 = acc_ref[0]
        a1 = acc_ref[1]
        o = (a0[:DA_HW] * (1.0 / a0[DA_HW:DA_HW + 1])
             - a1[:DA_HW] * (lam / a1[DA_HW:DA_HW + 1]))
        ms = jnp.mean(o * o, axis=0, keepdims=True)
        o = o * lax.rsqrt(ms + LN_EPS) * g_ref[...] * (1.0 - lam_init)
        o_ref[pl.ds(pl.multiple_of(qi * tq, tq), tq), :] = o.T.astype(BF16)
        reset()

    def half(qi, kt, cur, nxt):
        first_masked = q_mult * qi
        is_last = kt == first_masked + q_mult - 1
        nqi = jnp.where(is_last, jnp.minimum(qi + 1, nq - 1), qi)
        nkt = jnp.where(is_last, 0, kt + 1)
        next_d = jnp.where(is_last, -1, kt + 1 - first_masked)

        @pl.when(jnp.logical_and(jnp.logical_not(is_last), next_d < 0))
        def _():
            scores(nqi, nkt, *nxt, masked=None)
            consume(kt, *cur)

        for d in range(q_mult):
            @pl.when(next_d == d)
            def _():
                scores(nqi, nkt, *nxt, masked=d)
                consume(kt, *cur)

        @pl.when(is_last)
        def _():
            scores(nqi, nkt, *nxt, masked=None)
            consume(kt, *cur)
            finalize(qi)

        return nqi, nkt

    buf_a = (sa_ref, ta_ref)
    buf_b = (sb_ref, tb_ref)
    reset()
    scores(0, 0, *buf_a, masked=0)

    def pair(_, carry):
        qi, kt = carry
        qi, kt = half(qi, kt, buf_a, buf_b)
        return half(qi, kt, buf_b, buf_a)

    lax.fori_loop(0, n_tiles // 2, pair, (jnp.int32(0), jnp.int32(0)))


def _diffattn(qt, k3, vt, lq1, lk1, lq2, lk2, g_col, lam_init):
    B, H, hw, S = qt.shape
    tk = ATT_TILE
    tq = ATT_QMULT * tk
    vec = _const_spec((1, DA_DH))
    return pl.pallas_call(
        functools.partial(_diffattn_kernel, lam_init=lam_init, tk=tk, q_mult=ATT_QMULT),
        grid=(B, H),
        in_specs=[pl.BlockSpec((None, None, hw, S), lambda b, h: (b, h, 0, 0)),
                  pl.BlockSpec((None, S, hw), lambda b, h: (b, 0, h)),
                  pl.BlockSpec((None, None, VT_ROWS, S), lambda b, h: (b, h, 0, 0)),
                  vec, vec, vec, vec, _const_spec((hw, 1))],
        out_specs=pl.BlockSpec((None, S, hw), lambda b, h: (b, 0, h)),
        out_shape=jax.ShapeDtypeStruct((B, S, H * hw), BF16),
        scratch_shapes=[pltpu.VMEM((2, 1, tq), F32), pltpu.VMEM((2, VT_ROWS, tq), F32),
                        pltpu.VMEM((2, tk, tq), BF16), pltpu.VMEM((2, tk, tq), BF16),
                        pltpu.VMEM((2, 1, tq), F32), pltpu.VMEM((2, 1, tq), F32)],
        compiler_params=_params(2),
        name="diffattn",
    )(qt, k3, vt, lq1, lk1, lq2, lk2, g_col)


def _bias_kernel(f_ref, e_ref):
    ext, tq = e_ref.shape
    width = f_ref.shape[-1]
    g = jnp.broadcast_to(f_ref[...], (tq, width))
    g = pltpu.roll(g, width - REL_CLIP, axis=1, stride=1, stride_axis=0)
    bias = g[:, :ext] * LOG2E
    r = lax.broadcasted_iota(jnp.int32, (tq, ext), 0)
    u = lax.broadcasted_iota(jnp.int32, (tq, ext), 1)
    back = r // CHUNK + CB_LEFT - u // CHUNK
    e_ref[...] = jnp.where((back >= 0) & (back <= CB_LEFT), bias, NEG).T


def _bias_table(f_ext):
    H, width = f_ext.shape
    return pl.pallas_call(
        _bias_kernel,
        grid=(H,),
        in_specs=[pl.BlockSpec((None, 1, width), lambda h: (h, 0, 0))],
        out_specs=pl.BlockSpec((None, BAND_EXT, BAND_TQ), lambda h: (h, 0, 0)),
        out_shape=jax.ShapeDtypeStruct((H, BAND_EXT, BAND_TQ), F32),
        compiler_params=_params(1),
        name="bias_table",
    )(f_ext.reshape(H, 1, width))


def _bandattn_kernel(qt_ref, k_ref, vt_ref, e_ref, o_ref):
    tq = BAND_TQ
    lead = CB_LEFT * CHUNK
    row = lax.broadcasted_iota(jnp.int32, (LANES, tq), 0)
    chains = [(g, hh) for g in range(BAND_GROUP) for hh in range(2)]

    def window(g):
        want = (pl.program_id(2) * BAND_GROUP + g) * tq - lead
        start = pl.multiple_of(jnp.maximum(want, 0), tq)
        return start, pl.multiple_of(start - want, tq)

    def scores(g, hh):
        start, delta = window(g)
        kw = k_ref[pl.ds(start, BAND_WIN), :]
        qt = qt_ref[:, g * tq:(g + 1) * tq]
        sel = (row < CB_DH) if hh == 0 else (row >= CB_DH)
        return (_dot(kw, jnp.where(sel, qt, jnp.zeros_like(qt)))
                + e_ref[hh, pl.ds(delta, BAND_WIN), :])

    def softmax(s):
        return jnp.exp2((s - jnp.max(s, axis=0, keepdims=True)).astype(BF16))

    def pv(g, p):
        start, _ = window(g)
        r = _dot(vt_ref[:, pl.ds(start, BAND_WIN)], p)
        return r[:LANES] / r[LANES:LANES + 1]

    n = len(chains)
    s, p, o = {}, {}, {}
    s[0] = scores(*chains[0])
    s[1] = scores(*chains[1])
    p[0] = softmax(s.pop(0))
    for i in range(n):
        if i + 2 < n:
            s[i + 2] = scores(*chains[i + 2])
        if i + 1 < n:
            p[i + 1] = softmax(s.pop(i + 1))
        g, hh = chains[i]
        o[hh] = pv(g, p.pop(i))
        if hh == 1:
            both = jnp.where(row < CB_DH, o[0], o[1])
            o_ref[g * tq:(g + 1) * tq, :] = both.T.astype(BF16)


def _bandattn(qbt, kb3, vbt, e_tab):
    B, pairs, _, S = qbt.shape
    tq = BAND_TQ * BAND_GROUP
    return pl.pallas_call(
        _bandattn_kernel,
        grid=(B, pairs, S // tq),
        in_specs=[pl.BlockSpec((None, None, LANES, tq), lambda b, h, q: (b, h, 0, q)),
                  pl.BlockSpec((None, S, LANES), lambda b, h, q: (b, 0, h)),
                  pl.BlockSpec((None, None, VT_ROWS, S), lambda b, h, q: (b, h, 0, 0)),
                  pl.BlockSpec((2, BAND_EXT, BAND_TQ), lambda b, h, q: (h, 0, 0))],
        out_specs=pl.BlockSpec((None, tq, LANES), lambda b, h, q: (b, q, h)),
        out_shape=jax.ShapeDtypeStruct((B, S, pairs * LANES), BF16),
        compiler_params=_params(3),
        name="bandattn",
    )(qbt, kb3, vbt, e_tab)


def _layer_norm(z, g, b):
    mu = jnp.mean(z, axis=-1, keepdims=True)
    zc = z - mu
    var = jnp.mean(zc * zc, axis=-1, keepdims=True)
    return zc * lax.rsqrt(var + LN_EPS) * g + b


def _mixer_kernel(x_ref, oa_ref, ob_ref, wg_ref, bg_ref, wpa_ref, wpb_ref, wo_ref,
                  g_ref, b_ref, h_ref, *, alpha):
    d = x_ref.shape[1]
    sub = x_ref.shape[0] // MIX_SPLIT
    merged = []
    for r in range(MIX_SPLIT):
        rows = slice(r * sub, (r + 1) * sub)
        gates = _dot(x_ref[rows, :].astype(BF16), wg_ref[...]) + bg_ref[...]
        gates = 1.0 / (1.0 + jnp.exp(-gates))
        ya = _dot(oa_ref[rows, :], wpa_ref[...])
        yb = _dot(ob_ref[rows, :], wpb_ref[...])
        merged.append((gates[:, :d] * ya + gates[:, d:] * yb).astype(BF16))
    for r in range(MIX_SPLIT):
        rows = slice(r * sub, (r + 1) * sub)
        mix = _dot(merged[r], wo_ref[...])
        h_ref[rows, :] = _layer_norm(alpha * x_ref[rows, :] + mix, g_ref[...], b_ref[...])


def _mixer(x2, oa, ob, wg, bg, wpa, wpb, wo, g, b, alpha):
    T, D = x2.shape
    tm = ROW_TILE * MIX_SPLIT
    row = lambda w: pl.BlockSpec((tm, w), lambda i: (i, 0))
    return pl.pallas_call(
        functools.partial(_mixer_kernel, alpha=alpha),
        grid=(T // tm,),
        in_specs=[row(D), row(oa.shape[1]), row(ob.shape[1]), _const_spec(wg.shape),
                  _const_spec(bg.shape), _const_spec(wpa.shape), _const_spec(wpb.shape),
                  _const_spec(wo.shape), _const_spec(g.shape), _const_spec(b.shape)],
        out_specs=row(D),
        out_shape=jax.ShapeDtypeStruct((T, D), F32),
        compiler_params=_params(1),
        name="mixer",
    )(x2, oa, ob, wg, bg, wpa, wpb, wo, g, b)


def _convffn_kernel(h_ref, halo_ref, wu_ref, cw_ref, cb_ref, wd_ref, g_ref, b_ref,
                    o_ref, hx_ref, u_ref, a_ref, *, alpha, tiles_per_seq):
    i = pl.program_id(0)
    tm = h_ref.shape[0]
    dff = wd_ref.shape[0]
    hf = h_ref[...]
    keep = jnp.where(i % tiles_per_seq == 0, 0.0, 1.0)
    hx_ref[0:HALO, :] = (halo_ref[...] * keep).astype(BF16)
    hx_ref[HALO:, :] = hf.astype(BF16)
    n_chunks = dff // FF_CHUNK
    split = (n_chunks + 1) // 2 * FF_CHUNK

    def up(c):
        conv = []
        for part in range(2):
            c0 = part * dff + c * FF_CHUNK
            u = _dot(hx_ref[...], wu_ref[:, c0:c0 + FF_CHUNK])
            acc = cb_ref[:, c0:c0 + FF_CHUNK]
            for tap in range(CONV_W):
                shift = CONV_W - 1 - tap
                u_ref[tap, shift:shift + tm + HALO, :] = u
                acc = acc + u_ref[tap, HALO:HALO + tm, :] * cw_ref[tap:tap + 1, c0:c0 + FF_CHUNK]
            conv.append(acc)
        gate, val = conv
        act = gate * (1.0 / (1.0 + jnp.exp(-gate))) * val
        a_ref[:, c * FF_CHUNK:(c + 1) * FF_CHUNK] = act.astype(BF16)

    f = None
    for c in range(n_chunks):
        up(c)
        if (c - 1) * FF_CHUNK == split - FF_CHUNK:
            f = _dot(a_ref[:, :split], wd_ref[:split, :])
    f = f + _dot(a_ref[:, split:], wd_ref[split:, :])
    o_ref[...] = _layer_norm(alpha * hf + f, g_ref[...], b_ref[...])


def _convffn(h1, wu, cw, cb, wd, g, b, alpha, S):
    T, D = h1.shape
    tm = ROW_TILE
    dff = wd.shape[0]
    per_tile = tm // HALO
    return pl.pallas_call(
        functools.partial(_convffn_kernel, alpha=alpha, tiles_per_seq=S // tm),
        grid=(T // tm,),
        in_specs=[pl.BlockSpec((tm, D), lambda i: (i, 0)),
                  pl.BlockSpec((HALO, D), lambda i: (jnp.maximum(i * per_tile - 1, 0), 0)),
                  _const_spec(wu.shape), _const_spec(cw.shape), _const_spec(cb.shape),
                  _const_spec(wd.shape), _const_spec(g.shape), _const_spec(b.shape)],
        out_specs=pl.BlockSpec((tm, D), lambda i: (i, 0)),
        out_shape=jax.ShapeDtypeStruct((T, D), F32),
        scratch_shapes=[pltpu.VMEM((tm + HALO, D), BF16),
                        pltpu.VMEM((CONV_W, tm + 2 * HALO, FF_CHUNK), F32),
                        pltpu.VMEM((tm, dff), BF16)],
        compiler_params=_params(1),
        name="convffn",
    )(h1, h1, wu, cw, cb, wd, g, b)


def kernel(x, positions, w_in, b_gate, lambda_q1, lambda_k1, lambda_q2, lambda_k2, subln_g,
           rel_bias, w_proj_a, w_proj_b, w_out, ln1_g, ln1_b, w_up, conv_w, conv_b, w_down,
           ln2_g, ln2_b):
    B, S, D = x.shape
    depth = w_in.shape[0]
    alpha = (2 * depth) ** 0.25
    attn_cols = 3 * DA_HEADS * DA_HW + 3 * CB_HEADS * CB_DH
    freq = 1.0 / (ROPE_THETA ** (jnp.arange(0, DA_DH, 2, dtype=F32) / DA_DH))
    inv = jnp.tile(freq, LANES // freq.shape[0]).reshape(1, LANES)
    pos2 = positions.reshape(B * S, 1)
    rel_size = rel_bias.shape[-1]
    f_width = BAND_EXT + BAND_TQ
    row2 = lambda v: v.reshape(1, -1)

    h = x.reshape(B * S, D)
    for l in range(depth):
        lam_init = 0.8 - 0.6 * math.exp(-0.3 * l)
        qk_w = DA_HEADS * DA_HW
        w_attn = jnp.concatenate([_rope_layout(w_in[l, :, :qk_w]),
                                  _rope_layout(w_in[l, :, qk_w:2 * qk_w]),
                                  w_in[l, :, 2 * qk_w:attn_cols]], axis=1).astype(BF16)
        w_gate = w_in[l, :, attn_cols:].astype(BF16)
        qt, k2, vt, qbt, kb, vbt = _inproj(h, pos2, inv, w_attn, B, S)
        oa = _diffattn(qt, k2.reshape(B, S, -1), vt, row2(lambda_q1[l]), row2(lambda_k1[l]),
                       row2(lambda_q2[l]), row2(lambda_k2[l]), subln_g[l].reshape(-1, 1), lam_init)
        lead = CB_LEFT * CHUNK
        f_ext = jnp.pad(rel_bias[l], ((0, 0), (lead, f_width - lead - rel_size)), mode="edge")
        e_tab = _bias_table(f_ext)
        ob = _bandattn(qbt, kb.reshape(B, S, -1), vbt, e_tab)
        h = _mixer(h, oa.reshape(B * S, -1), ob.reshape(B * S, -1), w_gate,
                   row2(b_gate[l]), w_proj_a[l].astype(BF16), w_proj_b[l].astype(BF16),
                   w_out[l].astype(BF16), row2(ln1_g[l]), row2(ln1_b[l]), alpha)
        h = _convffn(h, w_up[l].astype(BF16), conv_w[l], row2(conv_b[l]), w_down[l].astype(BF16),
                     row2(ln2_g[l]), row2(ln2_b[l]), alpha, S)
    return h.reshape(B, S, D)
```

```python
import functools
import math

import jax
import jax.numpy as jnp
from jax import lax
from jax.experimental import pallas as pl
from jax.experimental.pallas import tpu as pltpu

F32 = jnp.float32
BF16 = jnp.bfloat16

LANES = 128
BF16_ROWS = 16
CHUNK = 64
DA_HEADS = 8
DA_DH = 64
DA_HW = 2 * DA_DH
CB_HEADS = 8
CB_DH = 64
CB_LEFT = 8
REL_CLIP = 256
ROPE_THETA = 10000.0
LN_EPS = 1e-5
CONV_W = 3
NEG = -1e30
LOG2E = math.log2(math.e)

VMEM_LIMIT = 56 * 1024 * 1024

ROW_TILE = 512
INPROJ_TILE = 1024
MIX_SPLIT = 2
ATT_TILE = 512
ATT_QMULT = 1
VT_ROWS = LANES + BF16_ROWS
BAND_TQ = 256
BAND_GROUP = 16
BAND_WIN = BAND_TQ + CB_LEFT * CHUNK
BAND_EXT = BAND_WIN + CB_LEFT * CHUNK
FF_CHUNK = 256
HALO = BF16_ROWS


def _dot(a, b):
    return jnp.dot(a, b, preferred_element_type=F32)


def _rope_layout(w):
    d = w.shape[0]
    return w.reshape(d, DA_HEADS, 2, 2, DA_DH // 2).transpose(0, 1, 3, 2, 4).reshape(d, -1)


def _const_spec(shape):
    nd = len(shape)
    return pl.BlockSpec(shape, lambda *_: (0,) * nd, pipeline_mode=pl.Buffered(1))


def _params(n_axes):
    return pltpu.CompilerParams(dimension_semantics=("arbitrary",) * n_axes,
                                vmem_limit_bytes=VMEM_LIMIT)


def _inproj_kernel(x_ref, pos_ref, inv_ref, w_ref,
                   qt_ref, k_ref, vt_ref, qbt_ref, kb_ref, vbt_ref, xb_ref, trig_ref,
                   *, scale_a, scale_b):
    tm = x_ref.shape[0]
    grp = 4 * LANES
    qk_w = DA_HEADS * DA_HW
    cbw = CB_HEADS * CB_DH
    lane = lax.broadcasted_iota(jnp.int32, (tm, LANES), 1)
    first_half = lane < LANES // 2
    step = pl.program_id(0)

    @pl.when(step >= 0)
    def _():
        xb = x_ref[...].astype(BF16)
        xb_ref[...] = xb
        ones = jnp.ones((BF16_ROWS, tm), BF16)
        ang = pos_ref[...].astype(F32) * inv_ref[...]
        sin = jnp.sin(ang)
        trig_ref[0] = jnp.cos(ang)
        trig_ref[1] = jnp.where(first_half, -sin, sin)
        for g in range(qk_w // grp):
            t = _dot(xb, w_ref[:, 2 * qk_w + g * grp: 2 * qk_w + (g + 1) * grp])
            for j in range(4):
                vt_ref[0, g * 4 + j, 0:LANES, :] = t[:, j * LANES:(j + 1) * LANES].T.astype(BF16)
                vt_ref[0, g * 4 + j, LANES:VT_ROWS, :] = ones
        off = 3 * qk_w
        t = _dot(xb, w_ref[:, off:off + cbw]) * scale_b
        for j in range(cbw // LANES):
            qbt_ref[0, j] = t[:, j * LANES:(j + 1) * LANES].T.astype(BF16)
        kb_ref[...] = _dot(xb, w_ref[:, off + cbw:off + 2 * cbw]).astype(BF16)
        t = _dot(xb, w_ref[:, off + 2 * cbw:off + 3 * cbw])
        for j in range(cbw // LANES):
            vbt_ref[0, j, 0:LANES, :] = t[:, j * LANES:(j + 1) * LANES].T.astype(BF16)
            vbt_ref[0, j, LANES:VT_ROWS, :] = ones

    @pl.when(step >= 0)
    def _():
        xb = xb_ref[...]
        cos = trig_ref[0]
        sin = trig_ref[1]

        def rope(t):
            return t * cos + pltpu.roll(t, LANES // 2, axis=1) * sin

        for g in range(qk_w // grp):
            t = _dot(xb, w_ref[:, g * grp:(g + 1) * grp])
            for j in range(4):
                r = rope(t[:, j * LANES:(j + 1) * LANES]) * scale_a
                qt_ref[0, g * 4 + j] = r.T.astype(BF16)
        for g in range(qk_w // grp):
            t = _dot(xb, w_ref[:, qk_w + g * grp: qk_w + (g + 1) * grp])
            for j in range(4):
                c0 = g * grp + j * LANES
                k_ref[:, c0:c0 + LANES] = rope(t[:, j * LANES:(j + 1) * LANES]).astype(BF16)


def _inproj(x2, pos2, inv, w_attn, B, S):
    T, D = x2.shape
    tm = INPROJ_TILE
    spb = S // tm
    qk_w = DA_HEADS * DA_HW
    cbw = CB_HEADS * CB_DH
    pairs = cbw // LANES

    def t_spec(heads, rows):
        return pl.BlockSpec((1, heads, rows, tm), lambda i: (i // spb, 0, 0, i % spb))

    def t_shape(heads, rows):
        return jax.ShapeDtypeStruct((B, heads, rows, S), BF16)

    row = lambda w: pl.BlockSpec((tm, w), lambda i: (i, 0))
    return pl.pallas_call(
        functools.partial(_inproj_kernel, scale_a=DA_DH ** -0.5 * LOG2E,
                          scale_b=CB_DH ** -0.5 * LOG2E),
        grid=(T // tm,),
        in_specs=[row(D), row(1), _const_spec((1, LANES)), _const_spec(w_attn.shape)],
        out_specs=[t_spec(DA_HEADS, LANES), row(qk_w), t_spec(DA_HEADS, VT_ROWS),
                   t_spec(pairs, LANES), row(cbw), t_spec(pairs, VT_ROWS)],
        out_shape=[t_shape(DA_HEADS, LANES), jax.ShapeDtypeStruct((T, qk_w), BF16),
                   t_shape(DA_HEADS, VT_ROWS), t_shape(pairs, LANES),
                   jax.ShapeDtypeStruct((T, cbw), BF16), t_shape(pairs, VT_ROWS)],
        scratch_shapes=[pltpu.VMEM((tm, D), BF16), pltpu.VMEM((2, tm, LANES), F32)],
        compiler_params=_params(1),
        name="inproj",
    )(x2, pos2, inv, w_attn)


def _diffattn_kernel(qt_ref, k_ref, vt_ref, lq1_ref, lk1_ref, lq2_ref, lk2_ref, g_ref,
                     o_ref, m_ref, acc_ref, sa_ref, sb_ref, ta_ref, tb_ref,
                     *, lam_init, tk, q_mult):
    S = qt_ref.shape[1]
    tq = q_mult * tk
    nq = S // tq
    n_tiles = q_mult * nq * (nq + 1) // 2
    assert n_tiles % 2 == 0
    row = lax.broadcasted_iota(jnp.int32, (DA_HW, tq), 0)
    kc = lax.broadcasted_iota(jnp.int32, (tk, tq), 0) // CHUNK
    qc = lax.broadcasted_iota(jnp.int32, (tk, tq), 1) // CHUNK
    visible = [kc + d * (tk // CHUNK) <= qc for d in range(q_mult)]

    def reset():
        m_ref[...] = jnp.full(m_ref.shape, NEG, F32)
        acc_ref[...] = jnp.zeros(acc_ref.shape, F32)

    def scores(qi, kt, s_ref, t_ref, masked):
        qt = qt_ref[:, pl.ds(pl.multiple_of(qi * tq, tq), tq)]
        kblk = k_ref[pl.ds(pl.multiple_of(kt * tk, tk), tk), :]
        zero = jnp.zeros_like(qt)
        for t in range(2):
            sub = (row // (DA_DH // 2)) % 2 == t
            qs = jnp.where(sub, qt, zero)
            if masked is None or q_mult != 1:
                s = _dot(kblk, qs)
            else:
                hk, hq = tk // 2, tq // 2
                late = _dot(kblk[hk:], qs[:, hq:])
                s = jnp.concatenate(
                    [_dot(kblk[:hk], qs),
                     jnp.concatenate([jnp.full((hk, hq), NEG, F32), late], axis=1)], axis=0)
            if masked is not None:
                s = jnp.where(visible[masked], s, NEG)
            s_ref[t] = s
            t_ref[t] = jnp.max(s, axis=0, keepdims=True)

    def consume(kt, s_ref, t_ref):
        vblk = vt_ref[:, pl.ds(pl.multiple_of(kt * tk, tk), tk)]
        for t in range(2):
            m_old = m_ref[t]
            m_new = jnp.maximum(m_old, t_ref[t])
            alpha = jnp.exp2(m_old - m_new)
            p = jnp.exp2((s_ref[t] - m_new).astype(BF16))
            acc_ref[t] = alpha * acc_ref[t] + _dot(vblk, p)
            m_ref[t] = m_new

    def finalize(qi):
        lam = (jnp.exp(jnp.sum(lq1_ref[...] * lk1_ref[...], axis=1, keepdims=True))
               - jnp.exp(jnp.sum(lq2_ref[...] * lk2_ref[...], axis=1, keepdims=True)) + lam_init)
        a0 = acc_ref[0]
        a1 = acc_ref[1]
        o = (a0[:DA_HW] * (1.0 / a0[DA_HW:DA_HW + 1])
             - a1[:DA_HW] * (lam / a1[DA_HW:DA_HW + 1]))
        ms = jnp.mean(o * o, axis=0, keepdims=True)
        o = o * lax.rsqrt(ms + LN_EPS) * g_ref[...] * (1.0 - lam_init)
        o_ref[pl.ds(pl.multiple_of(qi * tq, tq), tq), :] = o.T.astype(BF16)
        reset()

    def half(qi, kt, cur, nxt):
        first_masked = q_mult * qi
        is_last = kt == first_masked + q_mult - 1
        nqi = jnp.where(is_last, jnp.minimum(qi + 1, nq - 1), qi)
        nkt = jnp.where(is_last, 0, kt + 1)
        next_d = jnp.where(is_last, -1, kt + 1 - first_masked)

        @pl.when(jnp.logical_and(jnp.logical_not(is_last), next_d < 0))
        def _():
            scores(nqi, nkt, *nxt, masked=None)
            consume(kt, *cur)

        for d in range(q_mult):
            @pl.when(next_d == d)
            def _():
                scores(nqi, nkt, *nxt, masked=d)
                consume(kt, *cur)

        @pl.when(is_last)
        def _():
            scores(nqi, nkt, *nxt, masked=None)
            consume(kt, *cur)
            finalize(qi)

        return nqi, nkt

    buf_a = (sa_ref, ta_ref)
    buf_b = (sb_ref, tb_ref)
    reset()
    scores(0, 0, *buf_a, masked=0)

    def pair(_, carry):
        qi, kt = carry
        qi, kt = half(qi, kt, buf_a, buf_b)
        return half(qi, kt, buf_b, buf_a)

    lax.fori_loop(0, n_tiles // 2, pair, (jnp.int32(0), jnp.int32(0)))


def _diffattn(qt, k3, vt, lq1, lk1, lq2, lk2, g_col, lam_init):
    B, H, hw, S = qt.shape
    tk = ATT_TILE
    tq = ATT_QMULT * tk
    vec = _const_spec((1, DA_DH))
    return pl.pallas_call(
        functools.partial(_diffattn_kernel, lam_init=lam_init, tk=tk, q_mult=ATT_QMULT),
        grid=(B, H),
        in_specs=[pl.BlockSpec((None, None, hw, S), lambda b, h: (b, h, 0, 0)),
                  pl.BlockSpec((None, S, hw), lambda b, h: (b, 0, h)),
                  pl.BlockSpec((None, None, VT_ROWS, S), lambda b, h: (b, h, 0, 0)),
                  vec, vec, vec, vec, _const_spec((hw, 1))],
        out_specs=pl.BlockSpec((None, S, hw), lambda b, h: (b, 0, h)),
        out_shape=jax.ShapeDtypeStruct((B, S, H * hw), BF16),
        scratch_shapes=[pltpu.VMEM((2, 1, tq), F32), pltpu.VMEM((2, VT_ROWS, tq), F32),
                        pltpu.VMEM((2, tk, tq), F32), pltpu.VMEM((2, tk, tq), F32),
                        pltpu.VMEM((2, 1, tq), F32), pltpu.VMEM((2, 1, tq), F32)],
        compiler_params=_params(2),
        name="diffattn",
    )(qt, k3, vt, lq1, lk1, lq2, lk2, g_col)


def _bias_kernel(f_ref, e_ref):
    ext, tq = e_ref.shape
    width = f_ref.shape[-1]
    g = jnp.broadcast_to(f_ref[...], (tq, width))
    g = pltpu.roll(g, width - REL_CLIP, axis=1, stride=1, stride_axis=0)
    bias = g[:, :ext] * LOG2E
    r = lax.broadcasted_iota(jnp.int32, (tq, ext), 0)
    u = lax.broadcasted_iota(jnp.int32, (tq, ext), 1)
    back = r // CHUNK + CB_LEFT - u // CHUNK
    e_ref[...] = jnp.where((back >= 0) & (back <= CB_LEFT), bias, NEG).T


def _bias_table(f_ext):
    H, width = f_ext.shape
    return pl.pallas_call(
        _bias_kernel,
        grid=(H,),
        in_specs=[pl.BlockSpec((None, 1, width), lambda h: (h, 0, 0))],
        out_specs=pl.BlockSpec((None, BAND_EXT, BAND_TQ), lambda h: (h, 0, 0)),
        out_shape=jax.ShapeDtypeStruct((H, BAND_EXT, BAND_TQ), F32),
        compiler_params=_params(1),
        name="bias_table",
    )(f_ext.reshape(H, 1, width))


def _bandattn_kernel(qt_ref, k_ref, vt_ref, e_ref, o_ref):
    tq = BAND_TQ
    lead = CB_LEFT * CHUNK
    row = lax.broadcasted_iota(jnp.int32, (LANES, tq), 0)
    chains = [(g, hh) for g in range(BAND_GROUP) for hh in range(2)]

    def window(g):
        want = (pl.program_id(2) * BAND_GROUP + g) * tq - lead
        start = pl.multiple_of(jnp.maximum(want, 0), tq)
        return start, pl.multiple_of(start - want, tq)

    def scores(g, hh):
        start, delta = window(g)
        kw = k_ref[pl.ds(start, BAND_WIN), :]
        qt = qt_ref[:, g * tq:(g + 1) * tq]
        sel = (row < CB_DH) if hh == 0 else (row >= CB_DH)
        return (_dot(kw, jnp.where(sel, qt, jnp.zeros_like(qt)))
                + e_ref[hh, pl.ds(delta, BAND_WIN), :])

    def softmax(s):
        return jnp.exp2((s - jnp.max(s, axis=0, keepdims=True)).astype(BF16))

    def pv(g, p):
        start, _ = window(g)
        r = _dot(vt_ref[:, pl.ds(start, BAND_WIN)], p)
        return r[:LANES] / r[LANES:LANES + 1]

    n = len(chains)
    s, p, o = {}, {}, {}
    s[0] = scores(*chains[0])
    s[1] = scores(*chains[1])
    p[0] = softmax(s.pop(0))
    for i in range(n):
        if i + 2 < n:
            s[i + 2] = scores(*chains[i + 2])
        if i + 1 < n:
            p[i + 1] = softmax(s.pop(i + 1))
        g, hh = chains[i]
        o[hh] = pv(g, p.pop(i))
        if hh == 1:
            both = jnp.where(row < CB_DH, o[0], o[1])
            o_ref[g * tq:(g + 1) * tq, :] = both.T.astype(BF16)


def _bandattn(qbt, kb3, vbt, e_tab):
    B, pairs, _, S = qbt.shape
    tq = BAND_TQ * BAND_GROUP
    return pl.pallas_call(
        _bandattn_kernel,
        grid=(B, pairs, S // tq),
        in_specs=[pl.BlockSpec((None, None, LANES, tq), lambda b, h, q: (b, h, 0, q)),
                  pl.BlockSpec((None, S, LANES), lambda b, h, q: (b, 0, h)),
                  pl.BlockSpec((None, None, VT_ROWS, S), lambda b, h, q: (b, h, 0, 0)),
                  pl.BlockSpec((2, BAND_EXT, BAND_TQ), lambda b, h, q: (h, 0, 0))],
        out_specs=pl.BlockSpec((None, tq, LANES), lambda b, h, q: (b, q, h)),
        out_shape=jax.ShapeDtypeStruct((B, S, pairs * LANES), BF16),
        compiler_params=_params(3),
        name="bandattn",
    )(qbt, kb3, vbt, e_tab)


def _layer_norm(z, g, b):
    mu = jnp.mean(z, axis=-1, keepdims=True)
    zc = z - mu
    var = jnp.mean(zc * zc, axis=-1, keepdims=True)
    return zc * lax.rsqrt(var + LN_EPS) * g + b


def _mixer_kernel(x_ref, oa_ref, ob_ref, wg_ref, bg_ref, wpa_ref, wpb_ref, wo_ref,
                  g_ref, b_ref, h_ref, *, alpha):
    d = x_ref.shape[1]
    sub = x_ref.shape[0] // MIX_SPLIT
    merged = []
    for r in range(MIX_SPLIT):
        rows = slice(r * sub, (r + 1) * sub)
        gates = _dot(x_ref[rows, :].astype(BF16), wg_ref[...]) + bg_ref[...]
        gates = 1.0 / (1.0 + jnp.exp(-gates))
        ya = _dot(oa_ref[rows, :], wpa_ref[...])
        yb = _dot(ob_ref[rows, :], wpb_ref[...])
        merged.append((gates[:, :d] * ya + gates[:, d:] * yb).astype(BF16))
    for r in range(MIX_SPLIT):
        rows = slice(r * sub, (r + 1) * sub)
        mix = _dot(merged[r], wo_ref[...])
        h_ref[rows, :] = _layer_norm(alpha * x_ref[rows, :] + mix, g_ref[...], b_ref[...])


def _mixer(x2, oa, ob, wg, bg, wpa, wpb, wo, g, b, alpha):
    T, D = x2.shape
    tm = ROW_TILE * MIX_SPLIT
    row = lambda w: pl.BlockSpec((tm, w), lambda i: (i, 0))
    return pl.pallas_call(
        functools.partial(_mixer_kernel, alpha=alpha),
        grid=(T // tm,),
        in_specs=[row(D), row(oa.shape[1]), row(ob.shape[1]), _const_spec(wg.shape),
                  _const_spec(bg.shape), _const_spec(wpa.shape), _const_spec(wpb.shape),
                  _const_spec(wo.shape), _const_spec(g.shape), _const_spec(b.shape)],
        out_specs=row(D),
        out_shape=jax.ShapeDtypeStruct((T, D), F32),
        compiler_params=_params(1),
        name="mixer",
    )(x2, oa, ob, wg, bg, wpa, wpb, wo, g, b)


def _convffn_kernel(h_ref, halo_ref, wu_ref, cw_ref, cb_ref, wd_ref, g_ref, b_ref,
                    o_ref, hx_ref, u_ref, a_ref, *, alpha, tiles_per_seq):
    i = pl.program_id(0)
    tm = h_ref.shape[0]
    dff = wd_ref.shape[0]
    hf = h_ref[...]
    keep = jnp.where(i % tiles_per_seq == 0, 0.0, 1.0)
    hx_ref[0:HALO, :] = (halo_ref[...] * keep).astype(BF16)
    hx_ref[HALO:, :] = hf.astype(BF16)
    n_chunks = dff // FF_CHUNK
    split = (n_chunks + 1) // 2 * FF_CHUNK

    def up(c):
        conv = []
        for part in range(2):
            c0 = part * dff + c * FF_CHUNK
            u = _dot(hx_ref[...], wu_ref[:, c0:c0 + FF_CHUNK])
            acc = cb_ref[:, c0:c0 + FF_CHUNK]
            for tap in range(CONV_W):
                shift = CONV_W - 1 - tap
                u_ref[tap, shift:shift + tm + HALO, :] = u
                acc = acc + u_ref[tap, HALO:HALO + tm, :] * cw_ref[tap:tap + 1, c0:c0 + FF_CHUNK]
            conv.append(acc)
        gate, val = conv
        act = gate * (1.0 / (1.0 + jnp.exp(-gate))) * val
        a_ref[:, c * FF_CHUNK:(c + 1) * FF_CHUNK] = act.astype(BF16)

    f = None
    for c in range(n_chunks):
        up(c)
        if (c - 1) * FF_CHUNK == split - FF_CHUNK:
            f = _dot(a_ref[:, :split], wd_ref[:split, :])
    f = f + _dot(a_ref[:, split:], wd_ref[split:, :])
    o_ref[...] = _layer_norm(alpha * hf + f, g_ref[...], b_ref[...])


def _convffn(h1, wu, cw, cb, wd, g, b, alpha, S):
    T, D = h1.shape
    tm = ROW_TILE
    dff = wd.shape[0]
    per_tile = tm // HALO
    return pl.pallas_call(
        functools.partial(_convffn_kernel, alpha=alpha, tiles_per_seq=S // tm),
        grid=(T // tm,),
        in_specs=[pl.BlockSpec((tm, D), lambda i: (i, 0)),
                  pl.BlockSpec((HALO, D), lambda i: (jnp.maximum(i * per_tile - 1, 0), 0)),
                  _const_spec(wu.shape), _const_spec(cw.shape), _const_spec(cb.shape),
                  _const_spec(wd.shape), _const_spec(g.shape), _const_spec(b.shape)],
        out_specs=pl.BlockSpec((tm, D), lambda i: (i, 0)),
        out_shape=jax.ShapeDtypeStruct((T, D), F32),
        scratch_shapes=[pltpu.VMEM((tm + HALO, D), BF16),
                        pltpu.VMEM((CONV_W, tm + 2 * HALO, FF_CHUNK), F32),
                        pltpu.VMEM((tm, dff), BF16)],
        compiler_params=_params(1),
        name="convffn",
    )(h1, h1, wu, cw, cb, wd, g, b)


def kernel(x, positions, w_in, b_gate, lambda_q1, lambda_k1, lambda_q2, lambda_k2, subln_g,
           rel_bias, w_proj_a, w_proj_b, w_out, ln1_g, ln1_b, w_up, conv_w, conv_b, w_down,
           ln2_g, ln2_b):
    B, S, D = x.shape
    depth = w_in.shape[0]
    alpha = (2 * depth) ** 0.25
    attn_cols = 3 * DA_HEADS * DA_HW + 3 * CB_HEADS * CB_DH
    freq = 1.0 / (ROPE_THETA ** (jnp.arange(0, DA_DH, 2, dtype=F32) / DA_DH))
    inv = jnp.tile(freq, LANES // freq.shape[0]).reshape(1, LANES)
    pos2 = positions.reshape(B * S, 1)
    rel_size = rel_bias.shape[-1]
    f_width = BAND_EXT + BAND_TQ
    row2 = lambda v: v.reshape(1, -1)

    h = x.reshape(B * S, D)
    for l in range(depth):
        lam_init = 0.8 - 0.6 * math.exp(-0.3 * l)
        qk_w = DA_HEADS * DA_HW
        w_attn = jnp.concatenate([_rope_layout(w_in[l, :, :qk_w]),
                                  _rope_layout(w_in[l, :, qk_w:2 * qk_w]),
                                  w_in[l, :, 2 * qk_w:attn_cols]], axis=1).astype(BF16)
        w_gate = w_in[l, :, attn_cols:].astype(BF16)
        qt, k2, vt, qbt, kb, vbt = _inproj(h, pos2, inv, w_attn, B, S)
        oa = _diffattn(qt, k2.reshape(B, S, -1), vt, row2(lambda_q1[l]), row2(lambda_k1[l]),
                       row2(lambda_q2[l]), row2(lambda_k2[l]), subln_g[l].reshape(-1, 1), lam_init)
        lead = CB_LEFT * CHUNK
        f_ext = jnp.pad(rel_bias[l], ((0, 0), (lead, f_width - lead - rel_size)), mode="edge")
        e_tab = _bias_table(f_ext)
        ob = _bandattn(qbt, kb.reshape(B, S, -1), vbt, e_tab)
        h = _mixer(h, oa.reshape(B * S, -1), ob.reshape(B * S, -1), w_gate,
                   row2(b_gate[l]), w_proj_a[l].astype(BF16), w_proj_b[l].astype(BF16),
                   w_out[l].astype(BF16), row2(ln1_g[l]), row2(ln1_b[l]), alpha)
        h = _convffn(h, w_up[l].astype(BF16), conv_w[l], row2(conv_b[l]), w_down[l].astype(BF16),
                     row2(ln2_g[l]), row2(ln2_b[l]), alpha, S)
    return h.reshape(B, S, D)
```
